```python
import math
import jax, jax.numpy as jnp
from jax import lax
import numpy as np

D_MODEL = 2048
BATCH = 8
SEQ = 2048
DEPTH = 1

D_MIX = D_MODEL
D_SSM = D_MIX // 2
D_SGU = D_MIX - D_SSM
SSM_GROUP = 16
SSM_GROUPS = D_SSM // SSM_GROUP
SSM_STATE = 64
DT_MIN = 1e-3
DT_MAX = 1e-1
SGU_HEADS = 8
SGU_HEAD_DIM = D_SGU // SGU_HEADS
CHUNK = 128
D_IN = D_SSM + 2 * D_SGU
D_FF = 5632
CONV_W = 3
N_MOD = 6
EPS = 1e-6

kernel_name = "hybrid_s5_sgu_convffn_layer"


def rms_norm(x, g):
    xf = x.astype(jnp.float32)
    y = xf * lax.rsqrt(jnp.mean(xf * xf, axis=-1, keepdims=True) + EPS)
    return (y * g.astype(jnp.float32)).astype(x.dtype)


def layer_norm(x, g, b):
    xf = x.astype(jnp.float32)
    mu = jnp.mean(xf, axis=-1, keepdims=True)
    xc = xf - mu
    y = xc * lax.rsqrt(jnp.mean(xc * xc, axis=-1, keepdims=True) + EPS)
    return (y * g.astype(jnp.float32) + b.astype(jnp.float32)).astype(x.dtype)


def causal_dwconv(h, w, b):
    C = h.shape[-1]
    k = w.shape[0]
    out = lax.conv_general_dilated(h, w[:, None, :].astype(h.dtype), window_strides=(1,), padding=[(k - 1, 0)],
                                   dimension_numbers=("NWC", "WIO", "NWC"), feature_group_count=C)
    return out + b


def _ssm_combine(left, right):
    alr, ali, blr, bli = left
    arr, ari, brr, bri = right
    return (arr * alr - ari * ali,
            arr * ali + ari * alr,
            arr * blr - ari * bli + brr,
            arr * bli + ari * blr + bri)


def s5_mixer(u, log_dt, a_re, a_im, b_re, b_im, c_re, c_im, d, w_glu, b_glu):
    f32 = jnp.float32
    Bsz, S, _ = u.shape
    uf = u.astype(f32).reshape(Bsz, S, SSM_GROUPS, SSM_GROUP)
    dt = jnp.exp(log_dt.astype(f32))[:, None]
    are = a_re.astype(f32)
    aim = a_im.astype(f32)
    mag = jnp.exp(are * dt)
    lb_re = mag * jnp.cos(aim * dt)
    lb_im = mag * jnp.sin(aim * dt)
    den = are * are + aim * aim
    nr = lb_re - 1.0
    ni = lb_im
    f_re = ((nr * are + ni * aim) / den)[:, :, None]
    f_im = ((ni * are - nr * aim) / den)[:, :, None]
    br = b_re.astype(f32)
    bi = b_im.astype(f32)
    bb_re = f_re * br - f_im * bi
    bb_im = f_re * bi + f_im * br
    bu_re = jnp.einsum("bsgc,gnc->bsgn", uf, bb_re)
    bu_im = jnp.einsum("bsgc,gnc->bsgn", uf, bb_im)
    a_t_re = jnp.broadcast_to(lb_re[None, None], (1, S, SSM_GROUPS, SSM_STATE))
    a_t_im = jnp.broadcast_to(lb_im[None, None], (1, S, SSM_GROUPS, SSM_STATE))
    _, _, h_re, h_im = lax.associative_scan(_ssm_combine, (a_t_re, a_t_im, bu_re, bu_im), axis=1)
    y = (jnp.einsum("bsgn,gcn->bsgc", h_re, c_re.astype(f32))
         - jnp.einsum("bsgn,gcn->bsgc", h_im, c_im.astype(f32))
         + d.astype(f32).reshape(SSM_GROUPS, SSM_GROUP) * uf)
    y = jax.nn.gelu(y)
    y = y * jax.nn.sigmoid(jnp.einsum("bsgc,gce->bsge", y, w_glu.astype(f32)) + b_glu.astype(f32))
    return y.reshape(Bsz, S, D_SSM).astype(u.dtype)


def sgu_mixer(z_u, z_v, ln_g, ln_b, w_s, b_s):
    Bsz, S, _ = z_u.shape
    n_chunks = S // CHUNK
    u = jax.nn.gelu(z_u)
    v = layer_norm(jax.nn.gelu(z_v), ln_g, ln_b)
    v = v.reshape(Bsz, n_chunks, CHUNK, SGU_HEADS, SGU_HEAD_DIM)
    mask = jnp.tril(jnp.ones((CHUNK, CHUNK), dtype=bool))
    w = jnp.where(mask[None], w_s, 0.0).astype(v.dtype)
    mixed = jnp.einsum("hij,bcjhd->bcihd", w, v) + jnp.transpose(b_s)[None, None, :, :, None]
    return u * mixed.reshape(Bsz, S, D_SGU)


def setup_inputs(seed: int = 0) -> dict:
    key = jax.random.key(seed)
    ks = jax.random.split(key, 32)
    f32 = jnp.float32
    nrm = lambda k, shape, s: jax.random.normal(k, shape, f32) * s
    L = DEPTH
    G, N, C = SSM_GROUPS, SSM_STATE, SSM_GROUP
    n_idx = jnp.arange(N, dtype=f32)
    return {
        "x": nrm(ks[0], (BATCH, SEQ, D_MODEL), 1.0),
        "c": nrm(ks[1], (BATCH, D_MODEL), 1.0),
        "w_ada": nrm(ks[2], (L, D_MODEL, N_MOD * D_MODEL), 0.5 * D_MODEL ** -0.5),
        "b_ada": nrm(ks[3], (L, N_MOD * D_MODEL), 0.01),
        "g_pre_mix": 1.0 + nrm(ks[4], (L, D_MODEL), 0.01),
        "g_post_mix": 1.0 + nrm(ks[5], (L, D_MODEL), 0.01),
        "w_in": nrm(ks[6], (L, D_MODEL, D_IN), D_MODEL ** -0.5),
        "ssm_log_dt": jax.random.uniform(ks[7], (L, G), f32, math.log(DT_MIN), math.log(DT_MAX)),
        "ssm_a_re": -0.5 + nrm(ks[8], (L, G, N), 0.01),
        "ssm_a_im": math.pi * n_idx[None, None, :] + nrm(ks[9], (L, G, N), 0.01),
        "ssm_b_re": nrm(ks[10], (L, G, N, C), (2.0 * C) ** -0.5),
        "ssm_b_im": nrm(ks[11], (L, G, N, C), (2.0 * C) ** -0.5),
        "ssm_c_re": nrm(ks[12], (L, G, C, N), (2.0 * N) ** -0.5),
        "ssm_c_im": nrm(ks[13], (L, G, C, N), (2.0 * N) ** -0.5),
        "ssm_d": nrm(ks[14], (L, D_SSM), 1.0),
        "ssm_w_glu": nrm(ks[15], (L, G, C, C), C ** -0.5),
        "ssm_b_glu": nrm(ks[16], (L, G, C), 0.01),
        "sgu_ln_g": 1.0 + nrm(ks[17], (L, D_SGU), 0.01),
        "sgu_ln_b": nrm(ks[18], (L, D_SGU), 0.01),
        "sgu_w": nrm(ks[19], (L, SGU_HEADS, CHUNK, CHUNK), CHUNK ** -0.5),
        "sgu_b": 1.0 + nrm(ks[20], (L, SGU_HEADS, CHUNK), 0.01),
        "g_out_ssm": 1.0 + nrm(ks[21], (L, D_SSM), 0.01),
        "g_out_sgu": 1.0 + nrm(ks[22], (L, D_SGU), 0.01),
        "w_out": nrm(ks[23], (L, D_MIX, D_MODEL), D_MIX ** -0.5),
        "g_pre_ffn": 1.0 + nrm(ks[24], (L, D_MODEL), 0.01),
        "g_post_ffn": 1.0 + nrm(ks[25], (L, D_MODEL), 0.01),
        "w_up": nrm(ks[26], (L, D_MODEL, 2 * D_FF), D_MODEL ** -0.5),
        "conv_w": nrm(ks[27], (L, CONV_W, 2 * D_FF), CONV_W ** -0.5),
        "conv_b": nrm(ks[28], (L, 2 * D_FF), 0.01),
        "w_down": nrm(ks[29], (L, D_FF, D_MODEL), D_FF ** -0.5),
    }


def reference(x, c, w_ada, b_ada, g_pre_mix, g_post_mix, w_in, ssm_log_dt, ssm_a_re, ssm_a_im,
              ssm_b_re, ssm_b_im, ssm_c_re, ssm_c_im, ssm_d, ssm_w_glu, ssm_b_glu,
              sgu_ln_g, sgu_ln_b, sgu_w, sgu_b, g_out_ssm, g_out_sgu, w_out,
              g_pre_ffn, g_post_ffn, w_up, conv_w, conv_b, w_down):
    c_act = jax.nn.silu(c)
    for l in range(DEPTH):
        mod = jnp.einsum("bd,de->be", c_act, w_ada[l]) + b_ada[l]
        sh1, sc1, gt1, sh2, sc2, gt2 = jnp.split(mod[:, None, :], N_MOD, axis=-1)

        h = rms_norm(x, g_pre_mix[l]) * (1.0 + sc1) + sh1
        z = jnp.einsum("bsd,de->bse", h, w_in[l])
        z_ssm, z_u, z_v = jnp.split(z, [D_SSM, D_SSM + D_SGU], axis=-1)
        y_ssm = s5_mixer(z_ssm, ssm_log_dt[l], ssm_a_re[l], ssm_a_im[l], ssm_b_re[l], ssm_b_im[l],
                         ssm_c_re[l], ssm_c_im[l], ssm_d[l], ssm_w_glu[l], ssm_b_glu[l])
        y_sgu = sgu_mixer(z_u, z_v, sgu_ln_g[l], sgu_ln_b[l], sgu_w[l], sgu_b[l])
        y = jnp.concatenate([rms_norm(y_ssm, g_out_ssm[l]), rms_norm(y_sgu, g_out_sgu[l])], axis=-1)
        y = jnp.einsum("bse,ed->bsd", y, w_out[l])
        x = x + gt1 * rms_norm(y, g_post_mix[l])

        h = rms_norm(x, g_pre_ffn[l]) * (1.0 + sc2) + sh2
        up = causal_dwconv(jnp.einsum("bsd,df->bsf", h, w_up[l]), conv_w[l], conv_b[l])
        a, b = jnp.split(up, 2, axis=-1)
        f = jnp.einsum("bsf,fd->bsd", jax.nn.silu(a) * b, w_down[l])
        x = x + gt2 * rms_norm(f, g_post_ffn[l])
    return x
```

```python
import functools
import math

import jax
import jax.numpy as jnp
from jax import lax
from jax.experimental import pallas as pl
from jax.experimental.pallas import tpu as pltpu

F32 = jnp.float32
BF16 = jnp.bfloat16
EPS = 1e-6

LANES = 128
SSM_L = 16
SGU_CHUNK = 128
VMEM_LIMIT = 60 * 1024 * 1024


def _cparams(sem):
    return pltpu.CompilerParams(dimension_semantics=sem, vmem_limit_bytes=VMEM_LIMIT)


def _const_spec(shape):
    nd = len(shape)
    return pl.BlockSpec(shape, lambda *_: (0,) * nd, pipeline_mode=pl.Buffered(1))


def _rms(xf, g):
    return xf * lax.rsqrt(jnp.mean(xf * xf, axis=-1, keepdims=True) + EPS) * g


def _ada_kernel(c_ref, w_ref, b_ref, o_ref):
    c = c_ref[...]
    ca = (c * jax.nn.sigmoid(c)).astype(BF16)
    o_ref[...] = jnp.dot(ca, w_ref[...].astype(BF16), preferred_element_type=F32) + b_ref[...]


def _ada(c, w, b, tn=1024):
    bsz, d = c.shape
    n = w.shape[1]
    return pl.pallas_call(
        _ada_kernel,
        grid=(n // tn,),
        in_specs=[pl.BlockSpec((bsz, d), lambda j: (0, 0)),
                  pl.BlockSpec((d, tn), lambda j: (0, j)),
                  pl.BlockSpec((1, tn), lambda j: (0, j))],
        out_specs=pl.BlockSpec((bsz, tn), lambda j: (0, j)),
        out_shape=jax.ShapeDtypeStruct((bsz, n), F32),
        compiler_params=_cparams(("arbitrary",)),
        name="ada",
    )(c, w, b.reshape(1, n))


def _inproj_kernel(x_ref, sc_ref, sh_ref, g_ref, w_ref, zs_ref, zuv_ref, *, d_ssm):
    h = _rms(x_ref[0], g_ref[...]) * (1.0 + sc_ref[0]) + sh_ref[0]
    z = jnp.dot(h.astype(BF16), w_ref[...], preferred_element_type=F32)
    for j in range(d_ssm // LANES):
        zs_ref[0, j] = z[:, j * LANES:(j + 1) * LANES]
    zuv_ref[0] = z[:, d_ssm:].astype(zuv_ref.dtype)


def _inproj(x, sc, sh, g, w_bf16, d_ssm, tm=512):
    bsz, s, d = x.shape
    d_in = w_bf16.shape[1]
    nslab = d_ssm // LANES
    return pl.pallas_call(
        functools.partial(_inproj_kernel, d_ssm=d_ssm),
        grid=(bsz, s // tm),
        in_specs=[pl.BlockSpec((1, tm, d), lambda b, i: (b, i, 0)),
                  pl.BlockSpec((1, 1, d), lambda b, i: (b, 0, 0)),
                  pl.BlockSpec((1, 1, d), lambda b, i: (b, 0, 0)),
                  _const_spec((1, d)),
                  _const_spec((d, d_in))],
        out_specs=[pl.BlockSpec((1, nslab, tm, LANES), lambda b, i: (b, 0, i, 0)),
                   pl.BlockSpec((1, tm, d_in - d_ssm), lambda b, i: (b, i, 0))],
        out_shape=[jax.ShapeDtypeStruct((bsz, nslab, s, LANES), F32),
                   jax.ShapeDtypeStruct((bsz, s, d_in - d_ssm), BF16)],
        compiler_params=_cparams(("arbitrary", "arbitrary")),
        name="inproj",
    )(x, sc, sh, g.reshape(1, d), w_bf16)


def _ssm_tables(log_dt, a_re, a_im, b_re, b_im, c_re, c_im, d, w_glu):
    hp = lax.Precision.HIGHEST
    L = SSM_L
    G, N, C = b_re.shape
    dt = jnp.exp(log_dt)[:, None]
    ar_dt = a_re * dt
    ai_dt = a_im * dt
    mag = jnp.exp(ar_dt)
    lb_re = mag * jnp.cos(ai_dt)
    lb_im = mag * jnp.sin(ai_dt)
    den = a_re * a_re + a_im * a_im
    nr = lb_re - 1.0
    ni = lb_im
    f_re = ((nr * a_re + ni * a_im) / den)[:, :, None]
    f_im = ((ni * a_re - nr * a_im) / den)[:, :, None]
    bb_re = f_re * b_re - f_im * b_im
    bb_im = f_re * b_im + f_im * b_re
    tau = jnp.arange(L + 1, dtype=F32)[None, :, None]
    pmag = jnp.exp(ar_dt[:, None, :] * tau)
    lp_re = pmag * jnp.cos(ai_dt[:, None, :] * tau)
    lp_im = pmag * jnp.sin(ai_dt[:, None, :] * tau)
    lr = lp_re[:, :L, :, None]
    li = lp_im[:, :L, :, None]
    e_re = lr * bb_re[:, None] - li * bb_im[:, None]
    e_im = lr * bb_im[:, None] + li * bb_re[:, None]
    k = (jnp.einsum("gon,gtni->gtoi", c_re, e_re, precision=hp)
         - jnp.einsum("gon,gtni->gtoi", c_im, e_im, precision=hp))
    k = k.at[:, 0].add(d.reshape(G, C)[:, :, None] * jnp.eye(C, dtype=F32))
    s_idx = jnp.arange(L)[:, None]
    t_idx = jnp.arange(L)[None, :]
    lag = t_idx - s_idx
    kg = k[:, jnp.clip(lag, 0, L - 1)]
    kg = jnp.where((lag >= 0)[None, :, :, None, None], kg, 0.0)
    m = kg.transpose(0, 1, 4, 2, 3).reshape(G, L * C, L * C)
    p = jnp.concatenate([e_re[:, ::-1].transpose(0, 1, 3, 2),
                         e_im[:, ::-1].transpose(0, 1, 3, 2)], axis=-1).reshape(G, L * C, 2 * N)
    lr1 = lp_re[:, 1:, None, :]
    li1 = lp_im[:, 1:, None, :]
    d_re = c_re[:, None] * lr1 - c_im[:, None] * li1
    d_im = c_re[:, None] * li1 + c_im[:, None] * lr1
    q = jnp.concatenate([d_re, -d_im], axis=-1).transpose(0, 3, 1, 2).reshape(G, 2 * N, L * C)
    l_re = lp_re[:, L]
    l_im = lp_im[:, L]
    a1 = jnp.concatenate([l_re, l_re], axis=-1)
    a2u = jnp.concatenate([-l_im, l_im], axis=-1)
    a2v = jnp.concatenate([l_im, -l_im], axis=-1)
    gps = 2 * LANES // C
    wg = w_glu.reshape(G // gps, gps, C, C)
    wg = (wg[:, :, :, None, :] * jnp.eye(gps, dtype=F32)[None, :, None, :, None]).reshape(
        G // gps, gps * C, gps * C)
    return m.astype(BF16), p.astype(BF16), q.astype(BF16), a1, a2u, a2v, wg.astype(BF16)


def _ssm_kernel(zs_ref, m_ref, p_ref, q_ref, a1_ref, a2u_ref, a2v_ref, wg_ref, bg_ref, go_ref,
                o_ref, zf_ref, x_ref, ys_ref, hu_ref, hv_ref, *, n, pitch, groups, rblk, fin_rows):
    L = SSM_L
    nslab = zs_ref.shape[1]
    sb = n * L

    @pl.when(pl.program_id(1) == 0)
    def _():
        hu_ref[...] = jnp.zeros_like(hu_ref)
        hv_ref[...] = jnp.zeros_like(hv_ref)

    blk = lax.broadcasted_iota(jnp.int32, (rblk, LANES), 1) // L
    masks = {4: (blk % 8) < 4, 2: (blk % 4) < 2, 1: (blk % 2) < 1}

    def xpose8(a):
        for s in (4, 2, 1):
            sh = L * s
            b = list(a)
            for i in range(8):
                if (i // s) % 2 == 0:
                    lo, hi = a[i], a[i + s]
                    b[i] = jnp.where(masks[s], lo, pltpu.roll(hi, sh, 1))
                    b[i + s] = jnp.where(masks[s], pltpu.roll(lo, LANES - sh, 1), hi)
            a = b
        return a

    def flat_body(rb, carry):
        base = rb * (rblk * L)
        r0 = pl.multiple_of(rb * rblk, rblk)
        for j in range(nslab):
            zj = zs_ref.at[0, j]
            for hh in range(2):
                a = [zj[pl.ds(base + 8 * hh + tp, rblk, stride=L), :] for tp in range(8)]
                o = xpose8(a)
                for qq in range(8):
                    zf_ref[8 * j + qq, pl.ds(r0, rblk), hh * LANES:(hh + 1) * LANES] = o[qq].astype(BF16)
        return carry

    lax.fori_loop(0, n // rblk, flat_body, 0)

    def x_body(g, carry):
        xo = jnp.dot(zf_ref[g], p_ref[g], preferred_element_type=F32)
        x_ref[pl.ds(pl.multiple_of(g * pitch, 8), n), :] = xo
        return carry

    lax.fori_loop(0, groups, x_body, 0)

    a1 = a1_ref[...]
    a2u = a2u_ref[...]
    a2v = a2v_ref[...]

    def scan_body(kk, carry):
        hu, hv = carry
        u = x_ref[pl.ds(kk, groups, stride=pitch), :]
        x_ref[pl.ds(kk, groups, stride=pitch), :] = hu
        v = pltpu.roll(u, LANES // 2, 1)
        return a1 * hu + a2u * hv + u, a1 * hv + a2v * hu + v

    hu, hv = lax.fori_loop(0, n, scan_body, (hu_ref[...], hv_ref[...]))
    hu_ref[...] = hu
    hv_ref[...] = hv

    def y_body(g, carry):
        hprev = x_ref[pl.ds(pl.multiple_of(g * pitch, 8), n), :].astype(BF16)
        y = (jnp.dot(zf_ref[g], m_ref[g], preferred_element_type=F32)
             + jnp.dot(hprev, q_ref[g], preferred_element_type=F32))
        zf_ref[g] = jax.nn.gelu(y).astype(BF16)
        return carry

    lax.fori_loop(0, groups, y_body, 0)

    def unflat_body(rb, carry):
        base = rb * (rblk * L)
        r0 = pl.multiple_of(rb * rblk, rblk)
        for j in range(nslab):
            yj = ys_ref.at[j]
            for hh in range(2):
                a = [zf_ref[8 * j + qq, pl.ds(r0, rblk), hh * LANES:(hh + 1) * LANES].astype(F32)
                     for qq in range(8)]
                o = xpose8(a)
                for tp in range(8):
                    yj[pl.ds(base + 8 * hh + tp, rblk, stride=L), :] = o[tp]
        return carry

    lax.fori_loop(0, n // rblk, unflat_body, 0)

    def fin_body(rb, carry):
        rows = pl.ds(pl.multiple_of(rb * fin_rows, fin_rows), fin_rows)
        y = jnp.concatenate([ys_ref[j, rows, :] for j in range(nslab)], axis=1)
        yb = y.astype(BF16)
        w2 = 2 * LANES
        gate = jnp.concatenate(
            [jnp.dot(yb[:, i * w2:(i + 1) * w2], wg_ref[i], preferred_element_type=F32)
             for i in range(nslab // 2)], axis=1) + bg_ref[...]
        y = y * jax.nn.sigmoid(gate)
        o_ref[0, rows, :] = _rms(y, go_ref[...]).astype(o_ref.dtype)
        return carry

    lax.fori_loop(0, sb // fin_rows, fin_body, 0)


def _ssm(zs, tables, b_glu, g_out, sb=1024):
    m, p, q, a1, a2u, a2v, wg = tables
    bsz, nslab, s, _ = zs.shape
    groups = m.shape[0]
    d_ssm = nslab * LANES
    n = sb // SSM_L
    pitch = n + 8
    kern = functools.partial(_ssm_kernel, n=n, pitch=pitch, groups=groups, rblk=16, fin_rows=256)
    return pl.pallas_call(
        kern,
        grid=(bsz, s // sb),
        in_specs=[pl.BlockSpec((1, nslab, sb, LANES), lambda b, i: (b, 0, i, 0)),
                  _const_spec(m.shape), _const_spec(p.shape), _const_spec(q.shape),
                  _const_spec(a1.shape), _const_spec(a2u.shape), _const_spec(a2v.shape),
                  _const_spec(wg.shape), _const_spec((1, d_ssm)), _const_spec((1, d_ssm))],
        out_specs=pl.BlockSpec((1, sb, d_ssm), lambda b, i: (b, i, 0)),
        out_shape=jax.ShapeDtypeStruct((bsz, s, d_ssm), BF16),
        scratch_shapes=[pltpu.VMEM((groups, n, 2 * LANES), BF16),
                        pltpu.VMEM((groups * pitch, LANES), F32),
                        pltpu.VMEM((nslab, sb, LANES), F32),
                        pltpu.VMEM((groups, LANES), F32),
                        pltpu.VMEM((groups, LANES), F32)],
        compiler_params=_cparams(("arbitrary", "arbitrary")),
        name="ssm",
    )(zs, m, p, q, a1, a2u, a2v, wg, b_glu.reshape(1, d_ssm), g_out.reshape(1, d_ssm))


def _sgu_kernel(zuv_ref, lng_ref, lnb_ref, w_ref, bs_ref, go_ref, o_ref, *, heads, hd):
    d = heads * hd
    ts = zuv_ref.shape[1]
    nc = ts // SGU_CHUNK
    u = jax.nn.gelu(zuv_ref[0, :, :d].astype(F32))
    v = jax.nn.gelu(zuv_ref[0, :, d:].astype(F32))
    mu = jnp.mean(v, axis=-1, keepdims=True)
    vc = v - mu
    v = vc * lax.rsqrt(jnp.mean(vc * vc, axis=-1, keepdims=True) + EPS) * lng_ref[...] + lnb_ref[...]
    vb = v.astype(BF16)
    cols = []
    for h in range(heads):
        vh = jnp.concatenate([vb[c * SGU_CHUNK:(c + 1) * SGU_CHUNK, h * hd:(h + 1) * hd]
                              for c in range(nc)], axis=1)
        mixed = jnp.dot(w_ref[h], vh, preferred_element_type=F32)
        mixed = jnp.concatenate([mixed[:, c * hd:(c + 1) * hd] for c in range(nc)], axis=0)
        cols.append(mixed)
    mixed = jnp.concatenate(cols, axis=1)
    bias = jnp.concatenate([bs_ref[...]] * nc, axis=0)
    y = u * (mixed + bias)
    o_ref[0] = _rms(y, go_ref[...]).astype(o_ref.dtype)


def _sgu(zuv, ln_g, ln_b, w_masked_bf16, b_full, g_out, heads, ts=512):
    bsz, s, d2 = zuv.shape
    d = d2 // 2
    hd = d // heads
    return pl.pallas_call(
        functools.partial(_sgu_kernel, heads=heads, hd=hd),
        grid=(bsz, s // ts),
        in_specs=[pl.BlockSpec((1, ts, d2), lambda b, i: (b, i, 0)),
                  _const_spec((1, d)), _const_spec((1, d)),
                  _const_spec(w_masked_bf16.shape), _const_spec(b_full.shape), _const_spec((1, d))],
        out_specs=pl.BlockSpec((1, ts, d), lambda b, i: (b, i, 0)),
        out_shape=jax.ShapeDtypeStruct((bsz, s, d), BF16),
        compiler_params=_cparams(("arbitrary", "arbitrary")),
        name="sgu",
    )(zuv, ln_g.reshape(1, d), ln_b.reshape(1, d), w_masked_bf16, b_full, g_out.reshape(1, d))


def _outproj_kernel(ya_ref, yb_ref, x_ref, gt_ref, g_ref, wa_ref, wb_ref, o_ref):
    y = (jnp.dot(ya_ref[0], wa_ref[...], preferred_element_type=F32)
         + jnp.dot(yb_ref[0], wb_ref[...], preferred_element_type=F32))
    o_ref[0] = x_ref[0] + gt_ref[0] * _rms(y, g_ref[...])


def _outproj(ya, yb, x, gt, g, wa, wb, tm=512):
    bsz, s, d = x.shape
    da = ya.shape[-1]
    db = yb.shape[-1]
    return pl.pallas_call(
        _outproj_kernel,
        grid=(bsz, s // tm),
        in_specs=[pl.BlockSpec((1, tm, da), lambda b, i: (b, i, 0)),
                  pl.BlockSpec((1, tm, db), lambda b, i: (b, i, 0)),
                  pl.BlockSpec((1, tm, d), lambda b, i: (b, i, 0)),
                  pl.BlockSpec((1, 1, d), lambda b, i: (b, 0, 0)),
                  _const_spec((1, d)), _const_spec(wa.shape), _const_spec(wb.shape)],
        out_specs=pl.BlockSpec((1, tm, d), lambda b, i: (b, i, 0)),
        out_shape=jax.ShapeDtypeStruct((bsz, s, d), F32),
        compiler_params=_cparams(("arbitrary", "arbitrary")),
        name="outproj",
    )(ya, yb, x, gt, g.reshape(1, d), wa, wb)


FFN_HALO = 16


def _ffn_kernel(x_ref, xh_ref, sc_ref, sh_ref, gt_ref, gpre_ref, gpost_ref,
                wa_ref, wb_ref, cwa_ref, cwb_ref, cba_ref, cbb_ref, wd_ref,
                o_ref, h_ref, acc_ref):
    i = pl.program_id(1)
    j = pl.program_id(2)
    tm = x_ref.shape[1]

    @pl.when(j == 0)
    def _():
        mod = lambda xf: _rms(xf, gpre_ref[...]) * (1.0 + sc_ref[0]) + sh_ref[0]
        keep = jnp.where(i > 0, 1.0, 0.0).astype(F32)
        h_ref[:FFN_HALO, :] = (mod(xh_ref[0]) * keep).astype(BF16)
        h_ref[FFN_HALO:, :] = mod(x_ref[0]).astype(BF16)
        acc_ref[...] = jnp.zeros_like(acc_ref)

    h = h_ref[...]

    def branch(w_ref, cw_ref, cb_ref):
        up = jnp.dot(h, w_ref[...], preferred_element_type=F32)
        cw = cw_ref[...]
        r = (cw[0:1] * pltpu.roll(up, 2, 0) + cw[1:2] * pltpu.roll(up, 1, 0) + cw[2:3] * up)
        return r[FFN_HALO:] + cb_ref[...]

    a = branch(wa_ref, cwa_ref, cba_ref)
    b = branch(wb_ref, cwb_ref, cbb_ref)
    act = (a * jax.nn.sigmoid(a) * b).astype(BF16)
    acc_ref[...] += jnp.dot(act, wd_ref[...], preferred_element_type=F32)

    @pl.when(j == pl.num_programs(2) - 1)
    def _():
        o_ref[0] = x_ref[0] + gt_ref[0] * _rms(acc_ref[...], gpost_ref[...])


def _ffn(x, sc, sh, gt, g_pre, g_post, w_up_bf16, conv_w, conv_b, w_down_bf16, tm=512, tf=512):
    bsz, s, d = x.shape
    f = w_down_bf16.shape[0]
    nf = f // tf
    hb = tm // FFN_HALO
    cb = conv_b.reshape(1, 2 * f)
    return pl.pallas_call(
        _ffn_kernel,
        grid=(bsz, s // tm, nf),
        in_specs=[pl.BlockSpec((1, tm, d), lambda b, i, j: (b, i, 0)),
                  pl.BlockSpec((1, FFN_HALO, d), lambda b, i, j: (b, jnp.maximum(i * hb - 1, 0), 0)),
                  pl.BlockSpec((1, 1, d), lambda b, i, j: (b, 0, 0)),
                  pl.BlockSpec((1, 1, d), lambda b, i, j: (b, 0, 0)),
                  pl.BlockSpec((1, 1, d), lambda b, i, j: (b, 0, 0)),
                  _const_spec((1, d)), _const_spec((1, d)),
                  pl.BlockSpec((d, tf), lambda b, i, j: (0, j)),
                  pl.BlockSpec((d, tf), lambda b, i, j: (0, nf + j)),
                  pl.BlockSpec((3, tf), lambda b, i, j: (0, j)),
                  pl.BlockSpec((3, tf), lambda b, i, j: (0, nf + j)),
                  pl.BlockSpec((1, tf), lambda b, i, j: (0, j)),
                  pl.BlockSpec((1, tf), lambda b, i, j: (0, nf + j)),
                  pl.BlockSpec((tf, d), lambda b, i, j: (j, 0))],
        out_specs=pl.BlockSpec((1, tm, d), lambda b, i, j: (b, i, 0)),
        out_shape=jax.ShapeDtypeStruct((bsz, s, d), F32),
        scratch_shapes=[pltpu.VMEM((FFN_HALO + tm, d), BF16),
                        pltpu.VMEM((tm, d), F32)],
        compiler_params=_cparams(("arbitrary", "arbitrary", "arbitrary")),
        name="ffn",
    )(x, x, sc, sh, gt, g_pre.reshape(1, d), g_post.reshape(1, d),
      w_up_bf16, w_up_bf16, conv_w, conv_w, cb, cb, w_down_bf16)


def kernel(x, c, w_ada, b_ada, g_pre_mix, g_post_mix, w_in, ssm_log_dt, ssm_a_re, ssm_a_im, ssm_b_re, ssm_b_im, ssm_c_re, ssm_c_im, ssm_d, ssm_w_glu, ssm_b_glu, sgu_ln_g, sgu_ln_b, sgu_w, sgu_b, g_out_ssm, g_out_sgu, w_out, g_pre_ffn, g_post_ffn, w_up, conv_w, conv_b, w_down):
    depth = w_ada.shape[0]
    d = x.shape[-1]
    d_ssm = ssm_d.shape[-1]
    heads = sgu_w.shape[1]
    hd = (w_out.shape[1] - d_ssm) // heads
    for l in range(depth):
        mod = _ada(c, w_ada[l], b_ada[l])
        sh1, sc1, gt1, sh2, sc2, gt2 = [mod[:, None, k * d:(k + 1) * d] for k in range(6)]

        zs, zuv = _inproj(x, sc1, sh1, g_pre_mix[l], w_in[l].astype(BF16), d_ssm)
        tables = _ssm_tables(ssm_log_dt[l], ssm_a_re[l], ssm_a_im[l], ssm_b_re[l], ssm_b_im[l],
                             ssm_c_re[l], ssm_c_im[l], ssm_d[l], ssm_w_glu[l])
        y_ssm = _ssm(zs, tables, ssm_b_glu[l], g_out_ssm[l])
        w_s = jnp.tril(sgu_w[l]).astype(BF16)
        b_full = jnp.repeat(sgu_b[l].T, hd, axis=1)
        y_sgu = _sgu(zuv, sgu_ln_g[l], sgu_ln_b[l], w_s, b_full, g_out_sgu[l], heads)
        wo = w_out[l].astype(BF16)
        x = _outproj(y_ssm, y_sgu, x, gt1, g_post_mix[l], wo[:d_ssm], wo[d_ssm:])

        x = _ffn(x, sc2, sh2, gt2, g_pre_ffn[l], g_post_ffn[l], w_up[l].astype(BF16),
                 conv_w[l], conv_b[l], w_down[l].astype(BF16))
    return x
```

```python
import functools
import math

import jax
import jax.numpy as jnp
from jax import lax
from jax.experimental import pallas as pl
from jax.experimental.pallas import tpu as pltpu

F32 = jnp.float32
BF16 = jnp.bfloat16
EPS = 1e-6

LANES = 128
SSM_L = 16
SGU_CHUNK = 128
VMEM_LIMIT = 60 * 1024 * 1024


def _cparams(sem):
    return pltpu.CompilerParams(dimension_semantics=sem, vmem_limit_bytes=VMEM_LIMIT)


def _const_spec(shape):
    nd = len(shape)
    return pl.BlockSpec(shape, lambda *_: (0,) * nd, pipeline_mode=pl.Buffered(1))


def _rms(xf, g):
    return xf * lax.rsqrt(jnp.mean(xf * xf, axis=-1, keepdims=True) + EPS) * g


def _ada_kernel(c_ref, w_ref, b_ref, o_ref):
    c = c_ref[...]
    ca = (c * jax.nn.sigmoid(c)).astype(BF16)
    o_ref[...] = jnp.dot(ca, w_ref[...].astype(BF16), preferred_element_type=F32) + b_ref[...]


def _ada(c, w, b, tn=1024):
    bsz, d = c.shape
    n = w.shape[1]
    return pl.pallas_call(
        _ada_kernel,
        grid=(n // tn,),
        in_specs=[pl.BlockSpec((bsz, d), lambda j: (0, 0)),
                  pl.BlockSpec((d, tn), lambda j: (0, j)),
                  pl.BlockSpec((1, tn), lambda j: (0, j))],
        out_specs=pl.BlockSpec((bsz, tn), lambda j: (0, j)),
        out_shape=jax.ShapeDtypeStruct((bsz, n), F32),
        compiler_params=_cparams(("arbitrary",)),
        name="ada",
    )(c, w, b.reshape(1, n))


def _inproj_kernel(x_ref, sc_ref, sh_ref, g_ref, w_ref, zs_ref, zuv_ref, *, d_ssm):
    h = _rms(x_ref[0], g_ref[...]) * (1.0 + sc_ref[0]) + sh_ref[0]
    z = jnp.dot(h.astype(BF16), w_ref[...], preferred_element_type=F32)
    zs_ref[0] = z[:, :d_ssm].astype(zs_ref.dtype)
    zuv_ref[0] = z[:, d_ssm:].astype(zuv_ref.dtype)


def _inproj(x, sc, sh, g, w_bf16, d_ssm, tm=512):
    bsz, s, d = x.shape
    d_in = w_bf16.shape[1]
    return pl.pallas_call(
        functools.partial(_inproj_kernel, d_ssm=d_ssm),
        grid=(bsz, s // tm),
        in_specs=[pl.BlockSpec((1, tm, d), lambda b, i: (b, i, 0)),
                  pl.BlockSpec((1, 1, d), lambda b, i: (b, 0, 0)),
                  pl.BlockSpec((1, 1, d), lambda b, i: (b, 0, 0)),
                  _const_spec((1, d)),
                  _const_spec((d, d_in))],
        out_specs=[pl.BlockSpec((1, tm, d_ssm), lambda b, i: (b, i, 0)),
                   pl.BlockSpec((1, tm, d_in - d_ssm), lambda b, i: (b, i, 0))],
        out_shape=[jax.ShapeDtypeStruct((bsz, s, d_ssm), BF16),
                   jax.ShapeDtypeStruct((bsz, s, d_in - d_ssm), BF16)],
        compiler_params=_cparams(("arbitrary", "arbitrary")),
        name="inproj",
    )(x, sc, sh, g.reshape(1, d), w_bf16)


def _ssm_tables(log_dt, a_re, a_im, b_re, b_im, c_re, c_im, d, w_glu):
    hp = lax.Precision.HIGHEST
    L = SSM_L
    G, N, C = b_re.shape
    dt = jnp.exp(log_dt)[:, None]
    ar_dt = a_re * dt
    ai_dt = a_im * dt
    mag = jnp.exp(ar_dt)
    lb_re = mag * jnp.cos(ai_dt)
    lb_im = mag * jnp.sin(ai_dt)
    den = a_re * a_re + a_im * a_im
    nr = lb_re - 1.0
    ni = lb_im
    f_re = ((nr * a_re + ni * a_im) / den)[:, :, None]
    f_im = ((ni * a_re - nr * a_im) / den)[:, :, None]
    bb_re = f_re * b_re - f_im * b_im
    bb_im = f_re * b_im + f_im * b_re
    tau = jnp.arange(L + 1, dtype=F32)[None, :, None]
    pmag = jnp.exp(ar_dt[:, None, :] * tau)
    lp_re = pmag * jnp.cos(ai_dt[:, None, :] * tau)
    lp_im = pmag * jnp.sin(ai_dt[:, None, :] * tau)
    lr = lp_re[:, :L, :, None]
    li = lp_im[:, :L, :, None]
    e_re = lr * bb_re[:, None] - li * bb_im[:, None]
    e_im = lr * bb_im[:, None] + li * bb_re[:, None]
    k = (jnp.einsum("gon,gtni->gtoi", c_re, e_re, precision=hp)
         - jnp.einsum("gon,gtni->gtoi", c_im, e_im, precision=hp))
    k = k.at[:, 0].add(d.reshape(G, C)[:, :, None] * jnp.eye(C, dtype=F32))
    s_idx = jnp.arange(L)[:, None]
    t_idx = jnp.arange(L)[None, :]
    lag = t_idx - s_idx
    kg = k[:, jnp.clip(lag, 0, L - 1)]
    kg = jnp.where((lag >= 0)[None, :, :, None, None], kg, 0.0)
    m = kg.transpose(0, 1, 4, 2, 3).reshape(G, L * C, L * C)
    p = jnp.concatenate([e_re[:, ::-1].transpose(0, 1, 3, 2),
                         e_im[:, ::-1].transpose(0, 1, 3, 2)], axis=-1).reshape(G, L * C, 2 * N)
    lr1 = lp_re[:, 1:, None, :]
    li1 = lp_im[:, 1:, None, :]
    d_re = c_re[:, None] * lr1 - c_im[:, None] * li1
    d_im = c_re[:, None] * li1 + c_im[:, None] * lr1
    q = jnp.concatenate([d_re, -d_im], axis=-1).transpose(0, 3, 1, 2).reshape(G, 2 * N, L * C)
    l_re = lp_re[:, L]
    l_im = lp_im[:, L]
    a1 = jnp.concatenate([l_re, l_re], axis=-1)
    a2u = jnp.concatenate([-l_im, l_im], axis=-1)
    a2v = jnp.concatenate([l_im, -l_im], axis=-1)
    gps = 2 * LANES // C
    wg = w_glu.reshape(G // gps, gps, C, C)
    wg = (wg[:, :, :, None, :] * jnp.eye(gps, dtype=F32)[None, :, None, :, None]).reshape(
        G // gps, gps * C, gps * C)
    return m.astype(BF16), p.astype(BF16), q.astype(BF16), a1, a2u, a2v, wg.astype(BF16)


def _block_swap_perm():
    idx = jnp.arange(8 * LANES)
    a, b, c = idx // LANES, (idx // SSM_L) % 8, idx % SSM_L
    return (idx[None, :] == (b * LANES + a * SSM_L + c)[:, None]).astype(BF16)


def _ssm_kernel(zs_ref, perm_ref, m_ref, p_ref, q_ref, a1_ref, a2u_ref, a2v_ref, wg_ref, bg_ref, go_ref,
                o_ref, zf_ref, xu_ref, xv_ref, hp_ref, ys_ref, hu_ref, hv_ref, *, n, pitch, groups):
    L = SSM_L
    d = groups * L
    nslab = d // LANES

    @pl.when(pl.program_id(1) == 0)
    def _():
        hu_ref[...] = jnp.zeros_like(hu_ref)
        hv_ref[...] = jnp.zeros_like(hv_ref)

    def swap_blocks(pieces):
        lhs = jnp.concatenate([jnp.concatenate(ps, axis=1) for ps in pieces], axis=0)
        return jnp.dot(lhs, perm_ref[...], preferred_element_type=F32).astype(BF16)

    for hh in range(2):
        out = swap_blocks([[zs_ref[0, :, (8 * hh + tp) * d + j * LANES:(8 * hh + tp) * d + (j + 1) * LANES]
                            for tp in range(8)] for j in range(nslab)])
        for j in range(nslab):
            for qq in range(8):
                zf_ref[8 * j + qq, :, hh * LANES:(hh + 1) * LANES] = (
                    out[j * n:(j + 1) * n, qq * LANES:(qq + 1) * LANES])

    gx = 8

    def x_body(i, carry):
        for gi in range(gx):
            g = i * gx + gi
            xo = jnp.dot(zf_ref[g], p_ref[g], preferred_element_type=F32)
            r = pl.ds(pl.multiple_of(g * pitch, 8), n)
            xu_ref[r, :] = xo
            xv_ref[r, :] = pltpu.roll(xo, LANES // 2, 1)
        return carry

    lax.fori_loop(0, groups // gx, x_body, 0)

    a1 = a1_ref[...]
    a2u = a2u_ref[...]
    a2v = a2v_ref[...]
    ks = 8

    def scan_body(i, carry):
        hu, hv = carry
        for s in range(ks):
            kk = i * ks + s
            u = xu_ref[pl.ds(kk, groups, stride=pitch), :]
            v = xv_ref[pl.ds(kk, groups, stride=pitch), :]
            hp_ref[pl.ds(kk, groups, stride=pitch), :] = hu
            hu, hv = a1 * hu + a2u * hv + u, a1 * hv + a2v * hu + v
        return hu, hv

    hu, hv = lax.fori_loop(0, n // ks, scan_body, (hu_ref[...], hv_ref[...]))
    hu_ref[...] = hu
    hv_ref[...] = hv

    gy = 8

    def y_body(i, carry):
        for gi in range(gy):
            g = i * gy + gi
            hprev = hp_ref[pl.ds(pl.multiple_of(g * pitch, 8), n), :].astype(BF16)
            y = (jnp.dot(zf_ref[g], m_ref[g], preferred_element_type=F32)
                 + jnp.dot(hprev, q_ref[g], preferred_element_type=F32))
            zf_ref[g] = jax.nn.gelu(y).astype(BF16)
        return carry

    lax.fori_loop(0, groups // gy, y_body, 0)

    for hh in range(2):
        out = swap_blocks([[zf_ref[8 * j + qq, :, hh * LANES:(hh + 1) * LANES] for qq in range(8)]
                           for j in range(nslab)])
        for j in range(nslab):
            for tp in range(8):
                ys_ref[:, (8 * hh + tp) * d + j * LANES:(8 * hh + tp) * d + (j + 1) * LANES] = (
                    out[j * n:(j + 1) * n, tp * LANES:(tp + 1) * LANES])

    w2 = 2 * LANES
    for t in range(L):
        yb = ys_ref[:, t * d:(t + 1) * d]
        gate = jnp.concatenate(
            [jnp.dot(yb[:, i * w2:(i + 1) * w2], wg_ref[i], preferred_element_type=F32)
             for i in range(d // w2)], axis=1) + bg_ref[...]
        y = yb.astype(F32) * jax.nn.sigmoid(gate)
        o_ref[0, :, t * d:(t + 1) * d] = _rms(y, go_ref[...]).astype(o_ref.dtype)


def _ssm(zs, tables, b_glu, g_out, sb=1024):
    m, p, q, a1, a2u, a2v, wg = tables
    bsz, s, d_ssm = zs.shape
    groups = m.shape[0]
    L = SSM_L
    n = sb // L
    pitch = n + 8
    zs = zs.reshape(bsz, s // L, L * d_ssm)
    perm = _block_swap_perm()
    kern = functools.partial(_ssm_kernel, n=n, pitch=pitch, groups=groups)
    out = pl.pallas_call(
        kern,
        grid=(bsz, s // sb),
        in_specs=[pl.BlockSpec((1, n, L * d_ssm), lambda b, i: (b, i, 0)),
                  _const_spec(perm.shape),
                  _const_spec(m.shape), _const_spec(p.shape), _const_spec(q.shape),
                  _const_spec(a1.shape), _const_spec(a2u.shape), _const_spec(a2v.shape),
                  _const_spec(wg.shape), _const_spec((1, d_ssm)), _const_spec((1, d_ssm))],
        out_specs=pl.BlockSpec((1, n, L * d_ssm), lambda b, i: (b, i, 0)),
        out_shape=jax.ShapeDtypeStruct((bsz, s // L, L * d_ssm), BF16),
        scratch_shapes=[pltpu.VMEM((groups, n, 2 * LANES), BF16),
                        pltpu.VMEM((groups * pitch, LANES), F32),
                        pltpu.VMEM((groups * pitch, LANES), F32),
                        pltpu.VMEM((groups * pitch, LANES), F32),
                        pltpu.VMEM((n, L * d_ssm), BF16),
                        pltpu.VMEM((groups, LANES), F32),
                        pltpu.VMEM((groups, LANES), F32)],
        compiler_params=_cparams(("arbitrary", "arbitrary")),
        name="ssm",
    )(zs, perm, m, p, q, a1, a2u, a2v, wg, b_glu.reshape(1, d_ssm), g_out.reshape(1, d_ssm))
    return out.reshape(bsz, s, d_ssm)


def _sgu_kernel(zuv_ref, lng_ref, lnb_ref, w_ref, bs_ref, go_ref, o_ref, *, heads, hd):
    d = heads * hd
    ts = zuv_ref.shape[1]
    nc = ts // SGU_CHUNK
    u = jax.nn.gelu(zuv_ref[0, :, :d].astype(F32))
    v = jax.nn.gelu(zuv_ref[0, :, d:].astype(F32))
    mu = jnp.mean(v, axis=-1, keepdims=True)
    vc = v - mu
    v = vc * lax.rsqrt(jnp.mean(vc * vc, axis=-1, keepdims=True) + EPS) * lng_ref[...] + lnb_ref[...]
    vb = v.astype(BF16)
    cols = []
    for h in range(heads):
        vh = jnp.concatenate([vb[c * SGU_CHUNK:(c + 1) * SGU_CHUNK, h * hd:(h + 1) * hd]
                              for c in range(nc)], axis=1)
        mixed = jnp.dot(w_ref[h], vh, preferred_element_type=F32)
        mixed = jnp.concatenate([mixed[:, c * hd:(c + 1) * hd] for c in range(nc)], axis=0)
        cols.append(mixed)
    mixed = jnp.concatenate(cols, axis=1)
    bias = jnp.concatenate([bs_ref[...]] * nc, axis=0)
    y = u * (mixed + bias)
    o_ref[0] = _rms(y, go_ref[...]).astype(o_ref.dtype)


def _sgu(zuv, ln_g, ln_b, w_masked_bf16, b_full, g_out, heads, ts=512):
    bsz, s, d2 = zuv.shape
    d = d2 // 2
    hd = d // heads
    return pl.pallas_call(
        functools.partial(_sgu_kernel, heads=heads, hd=hd),
        grid=(bsz, s // ts),
        in_specs=[pl.BlockSpec((1, ts, d2), lambda b, i: (b, i, 0)),
                  _const_spec((1, d)), _const_spec((1, d)),
                  _const_spec(w_masked_bf16.shape), _const_spec(b_full.shape), _const_spec((1, d))],
        out_specs=pl.BlockSpec((1, ts, d), lambda b, i: (b, i, 0)),
        out_shape=jax.ShapeDtypeStruct((bsz, s, d), BF16),
        compiler_params=_cparams(("arbitrary", "arbitrary")),
        name="sgu",
    )(zuv, ln_g.reshape(1, d), ln_b.reshape(1, d), w_masked_bf16, b_full, g_out.reshape(1, d))


def _outproj_kernel(ya_ref, yb_ref, x_ref, gt_ref, g_ref, wa_ref, wb_ref, sc_ref, sh_ref, gn_ref,
                    o_ref, h_ref):
    y = (jnp.dot(ya_ref[0], wa_ref[...], preferred_element_type=F32)
         + jnp.dot(yb_ref[0], wb_ref[...], preferred_element_type=F32))
    x1 = x_ref[0] + gt_ref[0] * _rms(y, g_ref[...])
    o_ref[0] = x1
    h_ref[0] = (_rms(x1, gn_ref[...]) * (1.0 + sc_ref[0]) + sh_ref[0]).astype(h_ref.dtype)


def _outproj(ya, yb, x, gt, g, wa, wb, sc_next, sh_next, g_next, tm=512):
    bsz, s, d = x.shape
    da = ya.shape[-1]
    db = yb.shape[-1]
    row = lambda w: pl.BlockSpec((1, tm, w), lambda b, i: (b, i, 0))
    per_batch = pl.BlockSpec((1, 1, d), lambda b, i: (b, 0, 0))
    return pl.pallas_call(
        _outproj_kernel,
        grid=(bsz, s // tm),
        in_specs=[row(da), row(db), row(d), per_batch,
                  _const_spec((1, d)), _const_spec(wa.shape), _const_spec(wb.shape),
                  per_batch, per_batch, _const_spec((1, d))],
        out_specs=[row(d), row(d)],
        out_shape=[jax.ShapeDtypeStruct((bsz, s, d), F32), jax.ShapeDtypeStruct((bsz, s, d), BF16)],
        compiler_params=_cparams(("arbitrary", "arbitrary")),
        name="outproj",
    )(ya, yb, x, gt, g.reshape(1, d), wa, wb, sc_next, sh_next, g_next.reshape(1, d))


FFN_HALO = 16


def _ffn_kernel(x_ref, hin_ref, hh_ref, gt_ref, gpost_ref,
                wa_ref, wb_ref, cwa_ref, cwb_ref, cba_ref, cbb_ref, wd_ref,
                o_ref, h_ref, acc_ref):
    i = pl.program_id(1)
    j = pl.program_id(2)

    @pl.when(j == 0)
    def _():
        halo = hh_ref[0]
        h_ref[:FFN_HALO, :] = jnp.where(i > 0, halo, jnp.zeros_like(halo))
        h_ref[FFN_HALO:, :] = hin_ref[0]
        acc_ref[...] = jnp.zeros_like(acc_ref)

    h = h_ref[...]

    def branch(w_ref, cw_ref, cb_ref):
        up = jnp.dot(h, w_ref[...], preferred_element_type=F32)
        cw = cw_ref[...]
        r = (cw[0:1] * pltpu.roll(up, 2, 0) + cw[1:2] * pltpu.roll(up, 1, 0) + cw[2:3] * up)
        return r[FFN_HALO:] + cb_ref[...]

    a = branch(wa_ref, cwa_ref, cba_ref)
    b = branch(wb_ref, cwb_ref, cbb_ref)
    act = (a * jax.nn.sigmoid(a) * b).astype(BF16)
    acc_ref[...] += jnp.dot(act, wd_ref[...], preferred_element_type=F32)

    @pl.when(j == pl.num_programs(2) - 1)
    def _():
        o_ref[0] = x_ref[0] + gt_ref[0] * _rms(acc_ref[...], gpost_ref[...])


def _ffn(x, h, gt, g_post, w_up_bf16, conv_w, conv_b, w_down_bf16, tm=512, tf=512):
    bsz, s, d = x.shape
    f = w_down_bf16.shape[0]
    nf = f // tf
    hb = tm // FFN_HALO
    cb = conv_b.reshape(1, 2 * f)
    return pl.pallas_call(
        _ffn_kernel,
        grid=(bsz, s // tm, nf),
        in_specs=[pl.BlockSpec((1, tm, d), lambda b, i, j: (b, i, 0)),
                  pl.BlockSpec((1, tm, d), lambda b, i, j: (b, i, 0)),
                  pl.BlockSpec((1, FFN_HALO, d), lambda b, i, j: (b, jnp.maximum(i * hb - 1, 0), 0)),
                  pl.BlockSpec((1, 1, d), lambda b, i, j: (b, 0, 0)),
                  _const_spec((1, d)),
                  pl.BlockSpec((d, tf), lambda b, i, j: (0, j)),
                  pl.BlockSpec((d, tf), lambda b, i, j: (0, nf + j)),
                  pl.BlockSpec((3, tf), lambda b, i, j: (0, j)),
                  pl.BlockSpec((3, tf), lambda b, i, j: (0, nf + j)),
                  pl.BlockSpec((1, tf), lambda b, i, j: (0, j)),
                  pl.BlockSpec((1, tf), lambda b, i, j: (0, nf + j)),
                  pl.BlockSpec((tf, d), lambda b, i, j: (j, 0))],
        out_specs=pl.BlockSpec((1, tm, d), lambda b, i, j: (b, i, 0)),
        out_shape=jax.ShapeDtypeStruct((bsz, s, d), F32),
        scratch_shapes=[pltpu.VMEM((FFN_HALO + tm, d), BF16),
                        pltpu.VMEM((tm, d), F32)],
        compiler_params=_cparams(("arbitrary", "arbitrary", "arbitrary")),
        name="ffn",
    )(x, h, h, gt, g_post.reshape(1, d),
      w_up_bf16, w_up_bf16, conv_w, conv_w, cb, cb, w_down_bf16)


def kernel(x, c, w_ada, b_ada, g_pre_mix, g_post_mix, w_in, ssm_log_dt, ssm_a_re, ssm_a_im, ssm_b_re, ssm_b_im, ssm_c_re, ssm_c_im, ssm_d, ssm_w_glu, ssm_b_glu, sgu_ln_g, sgu_ln_b, sgu_w, sgu_b, g_out_ssm, g_out_sgu, w_out, g_pre_ffn, g_post_ffn, w_up, conv_w, conv_b, w_down):
    depth = w_ada.shape[0]
    d = x.shape[-1]
    d_ssm = ssm_d.shape[-1]
    heads = sgu_w.shape[1]
    hd = (w_out.shape[1] - d_ssm) // heads
    for l in range(depth):
        mod = _ada(c, w_ada[l], b_ada[l])
        sh1, sc1, gt1, sh2, sc2, gt2 = [mod[:, None, k * d:(k + 1) * d] for k in range(6)]

        zs, zuv = _inproj(x, sc1, sh1, g_pre_mix[l], w_in[l].astype(BF16), d_ssm)
        tables = _ssm_tables(ssm_log_dt[l], ssm_a_re[l], ssm_a_im[l], ssm_b_re[l], ssm_b_im[l],
                             ssm_c_re[l], ssm_c_im[l], ssm_d[l], ssm_w_glu[l])
        y_ssm = _ssm(zs, tables, ssm_b_glu[l], g_out_ssm[l])
        w_s = jnp.tril(sgu_w[l]).astype(BF16)
        b_full = jnp.repeat(sgu_b[l].T, hd, axis=1)
        y_sgu = _sgu(zuv, sgu_ln_g[l], sgu_ln_b[l], w_s, b_full, g_out_sgu[l], heads)
        wo = w_out[l].astype(BF16)
        x, h = _outproj(y_ssm, y_sgu, x, gt1, g_post_mix[l], wo[:d_ssm], wo[d_ssm:],
                        sc2, sh2, g_pre_ffn[l])

        x = _ffn(x, h, gt2, g_post_ffn[l], w_up[l].astype(BF16),
                 conv_w[l], conv_b[l], w_down[l].astype(BF16))
    return x
```

```python
import functools
import math

import jax
import jax.numpy as jnp
from jax import lax
from jax.experimental import pallas as pl
from jax.experimental.pallas import tpu as pltpu

F32 = jnp.float32
BF16 = jnp.bfloat16
EPS = 1e-6

LANES = 128
SSM_L = 16
SGU_CHUNK = 128
VMEM_LIMIT = 60 * 1024 * 1024


def _cparams(sem):
    return pltpu.CompilerParams(dimension_semantics=sem, vmem_limit_bytes=VMEM_LIMIT)


def _const_spec(shape):
    nd = len(shape)
    return pl.BlockSpec(shape, lambda *_: (0,) * nd, pipeline_mode=pl.Buffered(1))


def _rms(xf, g):
    return xf * lax.rsqrt(jnp.mean(xf * xf, axis=-1, keepdims=True) + EPS) * g


def _ada_kernel(c_ref, w_ref, b_ref, o_ref):
    c = c_ref[...]
    ca = (c * jax.nn.sigmoid(c)).astype(BF16)
    o_ref[...] = jnp.dot(ca, w_ref[...].astype(BF16), preferred_element_type=F32) + b_ref[...]


def _ada(c, w, b, tn=1024):
    bsz, d = c.shape
    n = w.shape[1]
    return pl.pallas_call(
        _ada_kernel,
        grid=(n // tn,),
        in_specs=[pl.BlockSpec((bsz, d), lambda j: (0, 0)),
                  pl.BlockSpec((d, tn), lambda j: (0, j)),
                  pl.BlockSpec((1, tn), lambda j: (0, j))],
        out_specs=pl.BlockSpec((bsz, tn), lambda j: (0, j)),
        out_shape=jax.ShapeDtypeStruct((bsz, n), F32),
        compiler_params=_cparams(("arbitrary",)),
        name="ada",
    )(c, w, b.reshape(1, n))


def _inproj_kernel(x_ref, sc_ref, sh_ref, g_ref, w_ref, zs_ref, zuv_ref, *, d_ssm):
    h = _rms(x_ref[0], g_ref[...]) * (1.0 + sc_ref[0]) + sh_ref[0]
    z = jnp.dot(h.astype(BF16), w_ref[...], preferred_element_type=F32)
    for j in range(d_ssm // LANES):
        zs_ref[0, j] = z[:, j * LANES:(j + 1) * LANES]
    zuv_ref[0] = z[:, d_ssm:].astype(zuv_ref.dtype)


def _inproj(x, sc, sh, g, w_bf16, d_ssm, tm=512):
    bsz, s, d = x.shape
    d_in = w_bf16.shape[1]
    nslab = d_ssm // LANES
    return pl.pallas_call(
        functools.partial(_inproj_kernel, d_ssm=d_ssm),
        grid=(bsz, s // tm),
        in_specs=[pl.BlockSpec((1, tm, d), lambda b, i: (b, i, 0)),
                  pl.BlockSpec((1, 1, d), lambda b, i: (b, 0, 0)),
                  pl.BlockSpec((1, 1, d), lambda b, i: (b, 0, 0)),
                  _const_spec((1, d)),
                  _const_spec((d, d_in))],
        out_specs=[pl.BlockSpec((1, nslab, tm, LANES), lambda b, i: (b, 0, i, 0)),
                   pl.BlockSpec((1, tm, d_in - d_ssm), lambda b, i: (b, i, 0))],
        out_shape=[jax.ShapeDtypeStruct((bsz, nslab, s, LANES), F32),
                   jax.ShapeDtypeStruct((bsz, s, d_in - d_ssm), BF16)],
        compiler_params=_cparams(("arbitrary", "arbitrary")),
        name="inproj",
    )(x, sc, sh, g.reshape(1, d), w_bf16)


def _toeplitz_kernel(k_ref, m_ref):
    gb, c, w = k_ref.shape
    lane = lax.broadcasted_iota(jnp.int32, (c, w), 1)
    for g in range(gb):
        kc = k_ref[g]
        for s in range(w // c):
            blk = kc if s == 0 else jnp.where(lane >= s * c, pltpu.roll(kc, s * c, 1), 0.0)
            m_ref[g, s * c:(s + 1) * c, :] = blk.astype(m_ref.dtype)


def _toeplitz(kcat, gb=8):
    g, c, w = kcat.shape
    return pl.pallas_call(
        _toeplitz_kernel,
        grid=(g // gb,),
        in_specs=[pl.BlockSpec((gb, c, w), lambda i: (i, 0, 0))],
        out_specs=pl.BlockSpec((gb, w, w), lambda i: (i, 0, 0)),
        out_shape=jax.ShapeDtypeStruct((g, w, w), BF16),
        compiler_params=_cparams(("arbitrary",)),
        name="toeplitz",
    )(kcat)


def _ssm_tables(log_dt, a_re, a_im, b_re, b_im, c_re, c_im, d, w_glu):
    hp = lax.Precision.HIGHEST
    L = SSM_L
    G, N, C = b_re.shape
    dt = jnp.exp(log_dt)[:, None]
    ar_dt = a_re * dt
    ai_dt = a_im * dt
    mag = jnp.exp(ar_dt)
    lb_re = mag * jnp.cos(ai_dt)
    lb_im = mag * jnp.sin(ai_dt)
    den = a_re * a_re + a_im * a_im
    nr = lb_re - 1.0
    ni = lb_im
    f_re = ((nr * a_re + ni * a_im) / den)[:, :, None]
    f_im = ((ni * a_re - nr * a_im) / den)[:, :, None]
    bb_re = f_re * b_re - f_im * b_im
    bb_im = f_re * b_im + f_im * b_re
    tau = jnp.arange(L + 1, dtype=F32)[None, :, None]
    pmag = jnp.exp(ar_dt[:, None, :] * tau)
    lp_re = pmag * jnp.cos(ai_dt[:, None, :] * tau)
    lp_im = pmag * jnp.sin(ai_dt[:, None, :] * tau)
    lq_re = lp_re[:, L - 1::-1, None, :]
    lq_im = lp_im[:, L - 1::-1, None, :]
    bt_re = bb_re.transpose(0, 2, 1)[:, None]
    bt_im = bb_im.transpose(0, 2, 1)[:, None]
    p = jnp.concatenate([lq_re * bt_re - lq_im * bt_im,
                         lq_re * bt_im + lq_im * bt_re], axis=-1).reshape(G, L * C, 2 * N)
    ct_re = c_re.transpose(0, 2, 1)[:, :, None, :]
    ct_im = c_im.transpose(0, 2, 1)[:, :, None, :]
    lt_re = lp_re.transpose(0, 2, 1)[:, :, :, None]
    lt_im = lp_im.transpose(0, 2, 1)[:, :, :, None]
    v_re = ct_re * lt_re - ct_im * lt_im
    v_im = ct_re * lt_im + ct_im * lt_re
    q = jnp.concatenate([v_re[:, :, 1:], -v_im[:, :, 1:]], axis=1).reshape(G, 2 * N, L * C)
    kcat = (jnp.einsum("gni,gnx->gix", bb_re, v_re[:, :, :L].reshape(G, N, L * C), precision=hp)
            - jnp.einsum("gni,gnx->gix", bb_im, v_im[:, :, :L].reshape(G, N, L * C), precision=hp))
    skip = d.reshape(G, C)[:, :, None] * jnp.eye(C, dtype=F32)
    kcat = kcat + jnp.pad(skip, ((0, 0), (0, 0), (0, (L - 1) * C)))
    m = _toeplitz(kcat)
    l_re = lp_re[:, L]
    l_im = lp_im[:, L]
    a1 = jnp.concatenate([l_re, l_re], axis=-1)
    a2u = jnp.concatenate([-l_im, l_im], axis=-1)
    a2v = jnp.concatenate([l_im, -l_im], axis=-1)
    gps = 2 * LANES // C
    wg = w_glu.reshape(G // gps, gps, C, C)
    wg = (wg[:, :, :, None, :] * jnp.eye(gps, dtype=F32)[None, :, None, :, None]).reshape(
        G // gps, gps * C, gps * C)
    return m, p.astype(BF16), q.astype(BF16), a1, a2u, a2v, wg.astype(BF16)


def _block_swap_perm():
    idx = jnp.arange(8 * LANES)
    a, b, c = idx // LANES, (idx // SSM_L) % 8, idx % SSM_L
    return (idx[None, :] == (b * LANES + a * SSM_L + c)[:, None]).astype(BF16)


def _ssm_kernel(zs_ref, perm_ref, m_ref, p_ref, q_ref, a1_ref, a2u_ref, a2v_ref, wg_ref, bg_ref, go_ref,
                o_ref, zf_ref, xu_ref, xv_ref, hp_ref, ys_ref, hu_ref, hv_ref, *, n, pitch, groups):
    L = SSM_L
    d = groups * L
    nslab = d // LANES

    @pl.when(pl.program_id(1) == 0)
    def _():
        hu_ref[...] = jnp.zeros_like(hu_ref)
        hv_ref[...] = jnp.zeros_like(hv_ref)

    def swap_blocks(pieces):
        lhs = jnp.concatenate([jnp.concatenate(ps, axis=1) for ps in pieces], axis=0)
        return jnp.dot(lhs, perm_ref[...], preferred_element_type=F32).astype(BF16)

    for hh in range(2):
        out = swap_blocks([[zs_ref[0, j, pl.ds(8 * hh + tp, n, stride=L), :].astype(BF16)
                            for tp in range(8)] for j in range(nslab)])
        for j in range(nslab):
            for qq in range(8):
                zf_ref[8 * j + qq, :, hh * LANES:(hh + 1) * LANES] = (
                    out[j * n:(j + 1) * n, qq * LANES:(qq + 1) * LANES])

    gx = 8

    def x_body(i, carry):
        for gi in range(gx):
            g = i * gx + gi
            xo = jnp.dot(zf_ref[g], p_ref[g], preferred_element_type=F32)
            r = pl.ds(pl.multiple_of(g * pitch, 8), n)
            xu_ref[r, :] = xo
            xv_ref[r, :] = pltpu.roll(xo, LANES // 2, 1)
        return carry

    lax.fori_loop(0, groups // gx, x_body, 0)

    a1 = a1_ref[...]
    a2u = a2u_ref[...]
    a2v = a2v_ref[...]
    ks = 8

    def scan_body(i, carry):
        hu, hv = carry
        for s in range(ks):
            kk = i * ks + s
            u = xu_ref[pl.ds(kk, groups, stride=pitch), :]
            v = xv_ref[pl.ds(kk, groups, stride=pitch), :]
            hp_ref[pl.ds(kk, groups, stride=pitch), :] = hu
            hu, hv = a1 * hu + a2u * hv + u, a1 * hv + a2v * hu + v
        return hu, hv

    hu, hv = lax.fori_loop(0, n // ks, scan_body, (hu_ref[...], hv_ref[...]))
    hu_ref[...] = hu
    hv_ref[...] = hv

    gy = 8

    def y_body(i, carry):
        for gi in range(gy):
            g = i * gy + gi
            hprev = hp_ref[pl.ds(pl.multiple_of(g * pitch, 8), n), :].astype(BF16)
            y = (jnp.dot(zf_ref[g], m_ref[g], preferred_element_type=F32)
                 + jnp.dot(hprev, q_ref[g], preferred_element_type=F32))
            zf_ref[g] = jax.nn.gelu(y).astype(BF16)
        return carry

    lax.fori_loop(0, groups // gy, y_body, 0)

    for hh in range(2):
        out = swap_blocks([[zf_ref[8 * j + qq, :, hh * LANES:(hh + 1) * LANES] for qq in range(8)]
                           for j in range(nslab)])
        for j in range(nslab):
            for tp in range(8):
                ys_ref[:, (8 * hh + tp) * d + j * LANES:(8 * hh + tp) * d + (j + 1) * LANES] = (
                    out[j * n:(j + 1) * n, tp * LANES:(tp + 1) * LANES])

    w2 = 2 * LANES
    for t in range(L):
        yb = ys_ref[:, t * d:(t + 1) * d]
        gate = jnp.concatenate(
            [jnp.dot(yb[:, i * w2:(i + 1) * w2], wg_ref[i], preferred_element_type=F32)
             for i in range(d // w2)], axis=1) + bg_ref[...]
        y = _rms(yb.astype(F32) * jax.nn.sigmoid(gate), go_ref[...])
        for j in range(nslab):
            o_ref[0, j, pl.ds(t, n, stride=L), :] = y[:, j * LANES:(j + 1) * LANES]


def _ssm(zs, tables, b_glu, g_out, sb=1024):
    m, p, q, a1, a2u, a2v, wg = tables
    bsz, nslab, s, _ = zs.shape
    d_ssm = nslab * LANES
    groups = m.shape[0]
    L = SSM_L
    n = sb // L
    pitch = n + 8
    perm = _block_swap_perm()
    kern = functools.partial(_ssm_kernel, n=n, pitch=pitch, groups=groups)
    slabs = pl.BlockSpec((1, nslab, sb, LANES), lambda b, i: (b, 0, i, 0))
    return pl.pallas_call(
        kern,
        grid=(bsz, s // sb),
        in_specs=[slabs,
                  _const_spec(perm.shape),
                  _const_spec(m.shape), _const_spec(p.shape), _const_spec(q.shape),
                  _const_spec(a1.shape), _const_spec(a2u.shape), _const_spec(a2v.shape),
                  _const_spec(wg.shape), _const_spec((1, d_ssm)), _const_spec((1, d_ssm))],
        out_specs=slabs,
        out_shape=jax.ShapeDtypeStruct((bsz, nslab, s, LANES), F32),
        scratch_shapes=[pltpu.VMEM((groups, n, 2 * LANES), BF16),
                        pltpu.VMEM((groups * pitch, LANES), F32),
                        pltpu.VMEM((groups * pitch, LANES), F32),
                        pltpu.VMEM((groups * pitch, LANES), F32),
                        pltpu.VMEM((n, L * d_ssm), BF16),
                        pltpu.VMEM((groups, LANES), F32),
                        pltpu.VMEM((groups, LANES), F32)],
        compiler_params=_cparams(("arbitrary", "arbitrary")),
        name="ssm",
    )(zs, perm, m, p, q, a1, a2u, a2v, wg, b_glu.reshape(1, d_ssm), g_out.reshape(1, d_ssm))


def _sgu_kernel(zuv_ref, lng_ref, lnb_ref, w_ref, bs_ref, go_ref, o_ref, *, heads, hd):
    d = heads * hd
    ts = zuv_ref.shape[1]
    nc = ts // SGU_CHUNK
    u = jax.nn.gelu(zuv_ref[0, :, :d].astype(F32))
    v = jax.nn.gelu(zuv_ref[0, :, d:].astype(F32))
    mu = jnp.mean(v, axis=-1, keepdims=True)
    vc = v - mu
    v = vc * lax.rsqrt(jnp.mean(vc * vc, axis=-1, keepdims=True) + EPS) * lng_ref[...] + lnb_ref[...]
    vb = v.astype(BF16)
    cols = []
    for h in range(heads):
        vh = jnp.concatenate([vb[c * SGU_CHUNK:(c + 1) * SGU_CHUNK, h * hd:(h + 1) * hd]
                              for c in range(nc)], axis=1)
        mixed = jnp.dot(w_ref[h], vh, preferred_element_type=F32)
        mixed = jnp.concatenate([mixed[:, c * hd:(c + 1) * hd] for c in range(nc)], axis=0)
        cols.append(mixed)
    mixed = jnp.concatenate(cols, axis=1)
    bias = jnp.concatenate([bs_ref[...]] * nc, axis=0)
    y = u * (mixed + bias)
    o_ref[0] = _rms(y, go_ref[...]).astype(o_ref.dtype)


def _sgu(zuv, ln_g, ln_b, w_masked_bf16, b_full, g_out, heads, ts=512):
    bsz, s, d2 = zuv.shape
    d = d2 // 2
    hd = d // heads
    return pl.pallas_call(
        functools.partial(_sgu_kernel, heads=heads, hd=hd),
        grid=(bsz, s // ts),
        in_specs=[pl.BlockSpec((1, ts, d2), lambda b, i: (b, i, 0)),
                  _const_spec((1, d)), _const_spec((1, d)),
                  _const_spec(w_masked_bf16.shape), _const_spec(b_full.shape), _const_spec((1, d))],
        out_specs=pl.BlockSpec((1, ts, d), lambda b, i: (b, i, 0)),
        out_shape=jax.ShapeDtypeStruct((bsz, s, d), BF16),
        compiler_params=_cparams(("arbitrary", "arbitrary")),
        name="sgu",
    )(zuv, ln_g.reshape(1, d), ln_b.reshape(1, d), w_masked_bf16, b_full, g_out.reshape(1, d))


def _outproj_kernel(ya_ref, yb_ref, x_ref, gt_ref, g_ref, wa_ref, wb_ref, sc_ref, sh_ref, gn_ref,
                    o_ref, h_ref):
    ya = jnp.concatenate([ya_ref[0, j] for j in range(ya_ref.shape[1])], axis=1).astype(BF16)
    y = (jnp.dot(ya, wa_ref[...], preferred_element_type=F32)
         + jnp.dot(yb_ref[0], wb_ref[...], preferred_element_type=F32))
    x1 = x_ref[0] + gt_ref[0] * _rms(y, g_ref[...])
    o_ref[0] = x1
    h_ref[0] = (_rms(x1, gn_ref[...]) * (1.0 + sc_ref[0]) + sh_ref[0]).astype(h_ref.dtype)


def _outproj(ya, yb, x, gt, g, wa, wb, sc_next, sh_next, g_next, tm=512):
    bsz, s, d = x.shape
    nslab = ya.shape[1]
    db = yb.shape[-1]
    row = lambda w: pl.BlockSpec((1, tm, w), lambda b, i: (b, i, 0))
    per_batch = pl.BlockSpec((1, 1, d), lambda b, i: (b, 0, 0))
    return pl.pallas_call(
        _outproj_kernel,
        grid=(bsz, s // tm),
        in_specs=[pl.BlockSpec((1, nslab, tm, LANES), lambda b, i: (b, 0, i, 0)), row(db), row(d), per_batch,
                  _const_spec((1, d)), _const_spec(wa.shape), _const_spec(wb.shape),
                  per_batch, per_batch, _const_spec((1, d))],
        out_specs=[row(d), row(d)],
        out_shape=[jax.ShapeDtypeStruct((bsz, s, d), F32), jax.ShapeDtypeStruct((bsz, s, d), BF16)],
        compiler_params=_cparams(("arbitrary", "arbitrary")),
        name="outproj",
    )(ya, yb, x, gt, g.reshape(1, d), wa, wb, sc_next, sh_next, g_next.reshape(1, d))


FFN_HALO = 16

def _ffn_up_kernel(hin_ref, hh_ref, wa_ref, wb_ref, cwa_ref, cwb_ref, cba_ref, cbb_ref, o_ref, h_ref):
    i = pl.program_id(1)
    j = pl.program_id(2)

    @pl.when(j == 0)
    def _():
        halo = hh_ref[0]
        h_ref[:FFN_HALO, :] = jnp.where(i > 0, halo, jnp.zeros_like(halo))
        h_ref[FFN_HALO:, :] = hin_ref[0]

    h = h_ref[...]

    def branch(w_ref, cw_ref, cb_ref):
        up = jnp.dot(h, w_ref[...], preferred_element_type=F32)
        cw = cw_ref[...]
        r = (cw[0:1] * pltpu.roll(up, 2, 0) + cw[1:2] * pltpu.roll(up, 1, 0) + cw[2:3] * up)
        return r[FFN_HALO:] + cb_ref[...]

    a = branch(wa_ref, cwa_ref, cba_ref)
    b = branch(wb_ref, cwb_ref, cbb_ref)
    o_ref[0] = (a * jax.nn.sigmoid(a) * b).astype(o_ref.dtype)


def _ffn_up(h, w_up_bf16, conv_w, conv_b, tm=1024, tf=512):
    bsz, s, d = h.shape
    f = w_up_bf16.shape[1] // 2
    nf = f // tf
    hb = tm // FFN_HALO
    cb = conv_b.reshape(1, 2 * f)
    return pl.pallas_call(
        _ffn_up_kernel,
        grid=(bsz, s // tm, nf),
        in_specs=[pl.BlockSpec((1, tm, d), lambda b, i, j: (b, i, 0)),
                  pl.BlockSpec((1, FFN_HALO, d), lambda b, i, j: (b, jnp.maximum(i * hb - 1, 0), 0)),
                  pl.BlockSpec((d, tf), lambda b, i, j: (0, j)),
                  pl.BlockSpec((d, tf), lambda b, i, j: (0, nf + j)),
                  pl.BlockSpec((3, tf), lambda b, i, j: (0, j)),
                  pl.BlockSpec((3, tf), lambda b, i, j: (0, nf + j)),
                  pl.BlockSpec((1, tf), lambda b, i, j: (0, j)),
                  pl.BlockSpec((1, tf), lambda b, i, j: (0, nf + j))],
        out_specs=pl.BlockSpec((1, tm, tf), lambda b, i, j: (b, i, j)),
        out_shape=jax.ShapeDtypeStruct((bsz, s, f), BF16),
        scratch_shapes=[pltpu.VMEM((FFN_HALO + tm, d), BF16)],
        compiler_params=_cparams(("arbitrary", "arbitrary", "arbitrary")),
        name="ffn_up",
    )(h, h, w_up_bf16, w_up_bf16, conv_w, conv_w, cb, cb)


def _ffn_down_kernel(act_ref, x_ref, gt_ref, g_ref, wd_ref, o_ref):
    f = jnp.dot(act_ref[0], wd_ref[...], preferred_element_type=F32)
    o_ref[0] = x_ref[0] + gt_ref[0] * _rms(f, g_ref[...])


def _ffn_down(act, x, gt, g_post, w_down_bf16, tm=256):
    bsz, s, d = x.shape
    f = act.shape[-1]
    return pl.pallas_call(
        _ffn_down_kernel,
        grid=(bsz, s // tm),
        in_specs=[pl.BlockSpec((1, tm, f), lambda b, i: (b, i, 0)),
                  pl.BlockSpec((1, tm, d), lambda b, i: (b, i, 0)),
                  pl.BlockSpec((1, 1, d), lambda b, i: (b, 0, 0)),
                  _const_spec((1, d)), _const_spec((f, d))],
        out_specs=pl.BlockSpec((1, tm, d), lambda b, i: (b, i, 0)),
        out_shape=jax.ShapeDtypeStruct((bsz, s, d), F32),
        compiler_params=_cparams(("arbitrary", "arbitrary")),
        name="ffn_down",
    )(act, x, gt, g_post.reshape(1, d), w_down_bf16)


def kernel(x, c, w_ada, b_ada, g_pre_mix, g_post_mix, w_in, ssm_log_dt, ssm_a_re, ssm_a_im, ssm_b_re, ssm_b_im, ssm_c_re, ssm_c_im, ssm_d, ssm_w_glu, ssm_b_glu, sgu_ln_g, sgu_ln_b, sgu_w, sgu_b, g_out_ssm, g_out_sgu, w_out, g_pre_ffn, g_post_ffn, w_up, conv_w, conv_b, w_down):
    depth = w_ada.shape[0]
    d = x.shape[-1]
    d_ssm = ssm_d.shape[-1]
    heads = sgu_w.shape[1]
    hd = (w_out.shape[1] - d_ssm) // heads
    for l in range(depth):
        mod = _ada(c, w_ada[l], b_ada[l])
        sh1, sc1, gt1, sh2, sc2, gt2 = [mod[:, None, k * d:(k + 1) * d] for k in range(6)]

        zs, zuv = _inproj(x, sc1, sh1, g_pre_mix[l], w_in[l].astype(BF16), d_ssm)
        tables = _ssm_tables(ssm_log_dt[l], ssm_a_re[l], ssm_a_im[l], ssm_b_re[l], ssm_b_im[l],
                             ssm_c_re[l], ssm_c_im[l], ssm_d[l], ssm_w_glu[l])
        y_ssm = _ssm(zs, tables, ssm_b_glu[l], g_out_ssm[l])
        w_s = jnp.tril(sgu_w[l]).astype(BF16)
        b_full = jnp.repeat(sgu_b[l].T, hd, axis=1)
        y_sgu = _sgu(zuv, sgu_ln_g[l], sgu_ln_b[l], w_s, b_full, g_out_sgu[l], heads)
        wo = w_out[l].astype(BF16)
        x, h = _outproj(y_ssm, y_sgu, x, gt1, g_post_mix[l], wo[:d_ssm], wo[d_ssm:],
                        sc2, sh2, g_pre_ffn[l])

        act = _ffn_up(h, w_up[l].astype(BF16), conv_w[l], conv_b[l])
        x = _ffn_down(act, x, gt2, g_post_ffn[l], w_down[l].astype(BF16))
    return x
```

```python
import functools
import math

import jax
import jax.numpy as jnp
from jax import lax
from jax.experimental import pallas as pl
from jax.experimental.pallas import tpu as pltpu

F32 = jnp.float32
BF16 = jnp.bfloat16
EPS = 1e-6

LANES = 128
SSM_L = 16
SGU_CHUNK = 128
VMEM_LIMIT = 60 * 1024 * 1024


def _cparams(sem):
    return pltpu.CompilerParams(dimension_semantics=sem, vmem_limit_bytes=VMEM_LIMIT)


def _const_spec(shape):
    nd = len(shape)
    return pl.BlockSpec(shape, lambda *_: (0,) * nd, pipeline_mode=pl.Buffered(1))


def _rms(xf, g):
    return xf * lax.rsqrt(jnp.mean(xf * xf, axis=-1, keepdims=True) + EPS) * g


def _ada_kernel(c_ref, w_ref, b_ref, o_ref):
    c = c_ref[...]
    ca = (c * jax.nn.sigmoid(c)).astype(BF16)
    o_ref[...] = jnp.dot(ca, w_ref[...].astype(BF16), preferred_element_type=F32) + b_ref[...]


def _ada(c, w, b, tn=1024):
    bsz, d = c.shape
    n = w.shape[1]
    return pl.pallas_call(
        _ada_kernel,
        grid=(n // tn,),
        in_specs=[pl.BlockSpec((bsz, d), lambda j: (0, 0)),
                  pl.BlockSpec((d, tn), lambda j: (0, j)),
                  pl.BlockSpec((1, tn), lambda j: (0, j))],
        out_specs=pl.BlockSpec((bsz, tn), lambda j: (0, j)),
        out_shape=jax.ShapeDtypeStruct((bsz, n), F32),
        compiler_params=_cparams(("arbitrary",)),
        name="ada",
    )(c, w, b.reshape(1, n))


def _inproj_kernel(x_ref, sc_ref, sh_ref, g_ref, w_ref, zs_ref, zuv_ref, *, d_ssm):
    h = _rms(x_ref[0], g_ref[...]) * (1.0 + sc_ref[0]) + sh_ref[0]
    z = jnp.dot(h.astype(BF16), w_ref[...], preferred_element_type=F32)
    for j in range(d_ssm // LANES):
        zs_ref[0, j] = z[:, j * LANES:(j + 1) * LANES]
    zuv_ref[0] = z[:, d_ssm:].astype(zuv_ref.dtype)


def _inproj(x, sc, sh, g, w_bf16, d_ssm, tm=512):
    bsz, s, d = x.shape
    d_in = w_bf16.shape[1]
    nslab = d_ssm // LANES
    return pl.pallas_call(
        functools.partial(_inproj_kernel, d_ssm=d_ssm),
        grid=(bsz, s // tm),
        in_specs=[pl.BlockSpec((1, tm, d), lambda b, i: (b, i, 0)),
                  pl.BlockSpec((1, 1, d), lambda b, i: (b, 0, 0)),
                  pl.BlockSpec((1, 1, d), lambda b, i: (b, 0, 0)),
                  _const_spec((1, d)),
                  _const_spec((d, d_in))],
        out_specs=[pl.BlockSpec((1, nslab, tm, LANES), lambda b, i: (b, 0, i, 0)),
                   pl.BlockSpec((1, tm, d_in - d_ssm), lambda b, i: (b, i, 0))],
        out_shape=[jax.ShapeDtypeStruct((bsz, nslab, s, LANES), F32),
                   jax.ShapeDtypeStruct((bsz, s, d_in - d_ssm), BF16)],
        compiler_params=_cparams(("arbitrary", "arbitrary")),
        name="inproj",
    )(x, sc, sh, g.reshape(1, d), w_bf16)


def _toeplitz_kernel(k_ref, m_ref):
    gb, c, w = k_ref.shape
    lane = lax.broadcasted_iota(jnp.int32, (c, w), 1)
    for g in range(gb):
        kc = k_ref[g]
        for s in range(w // c):
            blk = kc if s == 0 else jnp.where(lane >= s * c, pltpu.roll(kc, s * c, 1), 0.0)
            m_ref[g, s * c:(s + 1) * c, :] = blk.astype(m_ref.dtype)


def _toeplitz(kcat, gb=8):
    g, c, w = kcat.shape
    return pl.pallas_call(
        _toeplitz_kernel,
        grid=(g // gb,),
        in_specs=[pl.BlockSpec((gb, c, w), lambda i: (i, 0, 0))],
        out_specs=pl.BlockSpec((gb, w, w), lambda i: (i, 0, 0)),
        out_shape=jax.ShapeDtypeStruct((g, w, w), BF16),
        compiler_params=_cparams(("arbitrary",)),
        name="toeplitz",
    )(kcat)


def _ssm_tables(log_dt, a_re, a_im, b_re, b_im, c_re, c_im, d, w_glu):
    hp = lax.Precision.HIGHEST
    L = SSM_L
    G, N, C = b_re.shape
    dt = jnp.exp(log_dt)[:, None]
    ar_dt = a_re * dt
    ai_dt = a_im * dt
    mag = jnp.exp(ar_dt)
    lb_re = mag * jnp.cos(ai_dt)
    lb_im = mag * jnp.sin(ai_dt)
    den = a_re * a_re + a_im * a_im
    nr = lb_re - 1.0
    ni = lb_im
    f_re = ((nr * a_re + ni * a_im) / den)[:, :, None]
    f_im = ((ni * a_re - nr * a_im) / den)[:, :, None]
    bb_re = f_re * b_re - f_im * b_im
    bb_im = f_re * b_im + f_im * b_re
    tau = jnp.arange(L + 1, dtype=F32)[None, :, None]
    pmag = jnp.exp(ar_dt[:, None, :] * tau)
    lp_re = pmag * jnp.cos(ai_dt[:, None, :] * tau)
    lp_im = pmag * jnp.sin(ai_dt[:, None, :] * tau)
    lq_re = lp_re[:, L - 1::-1, None, :]
    lq_im = lp_im[:, L - 1::-1, None, :]
    bt_re = bb_re.transpose(0, 2, 1)[:, None]
    bt_im = bb_im.transpose(0, 2, 1)[:, None]
    p = jnp.concatenate([lq_re * bt_re - lq_im * bt_im,
                         lq_re * bt_im + lq_im * bt_re], axis=-1).reshape(G, L * C, 2 * N)
    ct_re = c_re.transpose(0, 2, 1)[:, :, None, :]
    ct_im = c_im.transpose(0, 2, 1)[:, :, None, :]
    lt_re = lp_re.transpose(0, 2, 1)[:, :, :, None]
    lt_im = lp_im.transpose(0, 2, 1)[:, :, :, None]
    v_re = ct_re * lt_re - ct_im * lt_im
    v_im = ct_re * lt_im + ct_im * lt_re
    q = jnp.concatenate([v_re[:, :, 1:], -v_im[:, :, 1:]], axis=1).reshape(G, 2 * N, L * C)
    kcat = (jnp.einsum("gni,gnx->gix", bb_re, v_re[:, :, :L].reshape(G, N, L * C), precision=hp)
            - jnp.einsum("gni,gnx->gix", bb_im, v_im[:, :, :L].reshape(G, N, L * C), precision=hp))
    skip = d.reshape(G, C)[:, :, None] * jnp.eye(C, dtype=F32)
    kcat = kcat + jnp.pad(skip, ((0, 0), (0, 0), (0, (L - 1) * C)))
    m = _toeplitz(kcat)
    l_re = lp_re[:, L]
    l_im = lp_im[:, L]
    a1 = jnp.concatenate([l_re, l_re], axis=-1)
    a2u = jnp.concatenate([-l_im, l_im], axis=-1)
    a2v = jnp.concatenate([l_im, -l_im], axis=-1)
    gps = 2 * LANES // C
    wg = w_glu.reshape(G // gps, gps, C, C)
    wg = (wg[:, :, :, None, :] * jnp.eye(gps, dtype=F32)[None, :, None, :, None]).reshape(
        G // gps, gps * C, gps * C)
    return m, p.astype(BF16), q.astype(BF16), a1, a2u, a2v, wg.astype(BF16)


def _block_swap_perm():
    idx = jnp.arange(8 * LANES)
    a, b, c = idx // LANES, (idx // SSM_L) % 8, idx % SSM_L
    return (idx[None, :] == (b * LANES + a * SSM_L + c)[:, None]).astype(BF16)


def _ssm_kernel(zs_ref, perm_ref, m_ref, p_ref, q_ref, a1_ref, a2u_ref, a2v_ref, wg_ref, bg_ref, go_ref,
                o_ref, zf_ref, xu_ref, xv_ref, hp_ref, ys_ref, hu_ref, hv_ref, *, n, pitch, groups):
    L = SSM_L
    d = groups * L
    nslab = d // LANES

    @pl.when(pl.program_id(1) == 0)
    def _():
        hu_ref[...] = jnp.zeros_like(hu_ref)
        hv_ref[...] = jnp.zeros_like(hv_ref)

    def swap_blocks(pieces):
        lhs = jnp.concatenate([jnp.concatenate(ps, axis=1) for ps in pieces], axis=0)
        return jnp.dot(lhs, perm_ref[...], preferred_element_type=F32).astype(BF16)

    for hh in range(2):
        out = swap_blocks([[zs_ref[0, j, pl.ds(8 * hh + tp, n, stride=L), :].astype(BF16)
                            for tp in range(8)] for j in range(nslab)])
        for j in range(nslab):
            for qq in range(8):
                zf_ref[8 * j + qq, :, hh * LANES:(hh + 1) * LANES] = (
                    out[j * n:(j + 1) * n, qq * LANES:(qq + 1) * LANES])

    gx = 8

    def x_body(i, carry):
        for gi in range(gx):
            g = i * gx + gi
            xo = jnp.dot(zf_ref[g], p_ref[g], preferred_element_type=F32)
            r = pl.ds(pl.multiple_of(g * pitch, 8), n)
            xu_ref[r, :] = xo
            xv_ref[r, :] = pltpu.roll(xo, LANES // 2, 1)
        return carry

    lax.fori_loop(0, groups // gx, x_body, 0)

    a1 = a1_ref[...]
    a2u = a2u_ref[...]
    a2v = a2v_ref[...]
    ks = 8

    def scan_body(i, carry):
        hu, hv = carry
        for s in range(ks):
            kk = i * ks + s
            u = xu_ref[pl.ds(kk, groups, stride=pitch), :]
            v = xv_ref[pl.ds(kk, groups, stride=pitch), :]
            hp_ref[pl.ds(kk, groups, stride=pitch), :] = hu
            hu, hv = a1 * hu + a2u * hv + u, a1 * hv + a2v * hu + v
        return hu, hv

    hu, hv = lax.fori_loop(0, n // ks, scan_body, (hu_ref[...], hv_ref[...]))
    hu_ref[...] = hu
    hv_ref[...] = hv

    gy = 8

    def y_body(i, carry):
        for gi in range(gy):
            g = i * gy + gi
            hprev = hp_ref[pl.ds(pl.multiple_of(g * pitch, 8), n), :].astype(BF16)
            y = (jnp.dot(zf_ref[g], m_ref[g], preferred_element_type=F32)
                 + jnp.dot(hprev, q_ref[g], preferred_element_type=F32))
            zf_ref[g] = jax.nn.gelu(y).astype(BF16)
        return carry

    lax.fori_loop(0, groups // gy, y_body, 0)

    for hh in range(2):
        out = swap_blocks([[zf_ref[8 * j + qq, :, hh * LANES:(hh + 1) * LANES] for qq in range(8)]
                           for j in range(nslab)])
        for j in range(nslab):
            for tp in range(8):
                ys_ref[:, (8 * hh + tp) * d + j * LANES:(8 * hh + tp) * d + (j + 1) * LANES] = (
                    out[j * n:(j + 1) * n, tp * LANES:(tp + 1) * LANES])

    w2 = 2 * LANES
    for t in range(L):
        yb = ys_ref[:, t * d:(t + 1) * d]
        gate = jnp.concatenate(
            [jnp.dot(yb[:, i * w2:(i + 1) * w2], wg_ref[i], preferred_element_type=F32)
             for i in range(d // w2)], axis=1) + bg_ref[...]
        y = _rms(yb.astype(F32) * jax.nn.sigmoid(gate), go_ref[...])
        for j in range(nslab):
            o_ref[0, j, pl.ds(t, n, stride=L), :] = y[:, j * LANES:(j + 1) * LANES]


def _ssm(zs, tables, b_glu, g_out, sb=1024):
    m, p, q, a1, a2u, a2v, wg = tables
    bsz, nslab, s, _ = zs.shape
    d_ssm = nslab * LANES
    groups = m.shape[0]
    L = SSM_L
    n = sb // L
    pitch = n + 8
    perm = _block_swap_perm()
    kern = functools.partial(_ssm_kernel, n=n, pitch=pitch, groups=groups)
    slabs = pl.BlockSpec((1, nslab, sb, LANES), lambda b, i: (b, 0, i, 0))
    return pl.pallas_call(
        kern,
        grid=(bsz, s // sb),
        in_specs=[slabs,
                  _const_spec(perm.shape),
                  _const_spec(m.shape), _const_spec(p.shape), _const_spec(q.shape),
                  _const_spec(a1.shape), _const_spec(a2u.shape), _const_spec(a2v.shape),
                  _const_spec(wg.shape), _const_spec((1, d_ssm)), _const_spec((1, d_ssm))],
        out_specs=slabs,
        out_shape=jax.ShapeDtypeStruct((bsz, nslab, s, LANES), F32),
        scratch_shapes=[pltpu.VMEM((groups, n, 2 * LANES), BF16),
                        pltpu.VMEM((groups * pitch, LANES), F32),
                        pltpu.VMEM((groups * pitch, LANES), F32),
                        pltpu.VMEM((groups * pitch, LANES), F32),
                        pltpu.VMEM((n, L * d_ssm), BF16),
                        pltpu.VMEM((groups, LANES), F32),
                        pltpu.VMEM((groups, LANES), F32)],
        compiler_params=_cparams(("arbitrary", "arbitrary")),
        name="ssm",
    )(zs, perm, m, p, q, a1, a2u, a2v, wg, b_glu.reshape(1, d_ssm), g_out.reshape(1, d_ssm))


def _sgu_kernel(zuv_ref, lng_ref, lnb_ref, w_ref, bs_ref, go_ref, o_ref, *, heads, hd):
    d = heads * hd
    ts = zuv_ref.shape[1]
    nc = ts // SGU_CHUNK
    u = jax.nn.gelu(zuv_ref[0, :, :d].astype(F32))
    v = jax.nn.gelu(zuv_ref[0, :, d:].astype(F32))
    mu = jnp.mean(v, axis=-1, keepdims=True)
    vc = v - mu
    v = vc * lax.rsqrt(jnp.mean(vc * vc, axis=-1, keepdims=True) + EPS) * lng_ref[...] + lnb_ref[...]
    vb = v.astype(BF16)
    cols = []
    for h in range(heads):
        vh = jnp.concatenate([vb[c * SGU_CHUNK:(c + 1) * SGU_CHUNK, h * hd:(h + 1) * hd]
                              for c in range(nc)], axis=1)
        mixed = jnp.dot(w_ref[h], vh, preferred_element_type=F32)
        mixed = jnp.concatenate([mixed[:, c * hd:(c + 1) * hd] for c in range(nc)], axis=0)
        cols.append(mixed)
    mixed = jnp.concatenate(cols, axis=1)
    bias = jnp.concatenate([bs_ref[...]] * nc, axis=0)
    y = u * (mixed + bias)
    o_ref[0] = _rms(y, go_ref[...]).astype(o_ref.dtype)


def _sgu(zuv, ln_g, ln_b, w_masked_bf16, b_full, g_out, heads, ts=512):
    bsz, s, d2 = zuv.shape
    d = d2 // 2
    hd = d // heads
    return pl.pallas_call(
        functools.partial(_sgu_kernel, heads=heads, hd=hd),
        grid=(bsz, s // ts),
        in_specs=[pl.BlockSpec((1, ts, d2), lambda b, i: (b, i, 0)),
                  _const_spec((1, d)), _const_spec((1, d)),
                  _const_spec(w_masked_bf16.shape), _const_spec(b_full.shape), _const_spec((1, d))],
        out_specs=pl.BlockSpec((1, ts, d), lambda b, i: (b, i, 0)),
        out_shape=jax.ShapeDtypeStruct((bsz, s, d), BF16),
        compiler_params=_cparams(("arbitrary", "arbitrary")),
        name="sgu",
    )(zuv, ln_g.reshape(1, d), ln_b.reshape(1, d), w_masked_bf16, b_full, g_out.reshape(1, d))


def _outproj_kernel(ya_ref, yb_ref, x_ref, gt_ref, g_ref, wa_ref, wb_ref, sc_ref, sh_ref, gn_ref,
                    o_ref, h_ref):
    ya = jnp.concatenate([ya_ref[0, j] for j in range(ya_ref.shape[1])], axis=1).astype(BF16)
    y = (jnp.dot(ya, wa_ref[...], preferred_element_type=F32)
         + jnp.dot(yb_ref[0], wb_ref[...], preferred_element_type=F32))
    x1 = x_ref[0] + gt_ref[0] * _rms(y, g_ref[...])
    o_ref[0] = x1
    h_ref[0] = (_rms(x1, gn_ref[...]) * (1.0 + sc_ref[0]) + sh_ref[0]).astype(h_ref.dtype)


def _outproj(ya, yb, x, gt, g, wa, wb, sc_next, sh_next, g_next, tm=512):
    bsz, s, d = x.shape
    nslab = ya.shape[1]
    db = yb.shape[-1]
    row = lambda w: pl.BlockSpec((1, tm, w), lambda b, i: (b, i, 0))
    per_batch = pl.BlockSpec((1, 1, d), lambda b, i: (b, 0, 0))
    return pl.pallas_call(
        _outproj_kernel,
        grid=(bsz, s // tm),
        in_specs=[pl.BlockSpec((1, nslab, tm, LANES), lambda b, i: (b, 0, i, 0)), row(db), row(d), per_batch,
                  _const_spec((1, d)), _const_spec(wa.shape), _const_spec(wb.shape),
                  per_batch, per_batch, _const_spec((1, d))],
        out_specs=[row(d), row(d)],
        out_shape=[jax.ShapeDtypeStruct((bsz, s, d), F32), jax.ShapeDtypeStruct((bsz, s, d), BF16)],
        compiler_params=_cparams(("arbitrary", "arbitrary")),
        name="outproj",
    )(ya, yb, x, gt, g.reshape(1, d), wa, wb, sc_next, sh_next, g_next.reshape(1, d))


FFN_HALO = 16
FFN_CONV_ROWS = 64

def _ffn_up_kernel(hin_ref, hh_ref, wa_ref, wb_ref, cwa_ref, cwb_ref, cba_ref, cbb_ref, o_ref,
                   h_ref, ua0_ref, ub0_ref, ua1_ref, ub1_ref, *, nt, nf):
    s = pl.program_id(0)
    tm = o_ref.shape[1]
    p = jnp.minimum(s, pl.num_programs(0) - 2)
    j = lax.rem(p, nf)
    i = lax.rem(lax.div(p, nf), nt)

    @pl.when(j == 0)
    def _():
        halo = hh_ref[0]
        h_ref[:FFN_HALO, :] = jnp.where(i > 0, halo, jnp.zeros_like(halo))
        h_ref[FFN_HALO:, :] = hin_ref[0]

    @pl.when(s == 0)
    def _():
        ua1_ref[...] = jnp.zeros_like(ua1_ref)
        ub1_ref[...] = jnp.zeros_like(ub1_ref)

    def step(dst_a, dst_b, src_a, src_b):
        for r0 in range(0, tm, FFN_CONV_ROWS):
            def conv(u_ref, cw_ref, cb_ref):
                cw = cw_ref[...]
                taps = [u_ref[pl.ds(FFN_HALO - 2 + k + r0, FFN_CONV_ROWS), :] for k in range(3)]
                return cw[0:1] * taps[0] + cw[1:2] * taps[1] + cw[2:3] * taps[2] + cb_ref[...]

            a = conv(src_a, cwa_ref, cba_ref)
            b = conv(src_b, cwb_ref, cbb_ref)
            o_ref[0, r0:r0 + FFN_CONV_ROWS, :] = (a * jax.nn.sigmoid(a) * b).astype(o_ref.dtype)
        h = h_ref[...]
        dst_a[...] = jnp.dot(h, wa_ref[...], preferred_element_type=F32)
        dst_b[...] = jnp.dot(h, wb_ref[...], preferred_element_type=F32)

    parity = lax.rem(s, 2)

    @pl.when(parity == 0)
    def _():
        step(ua0_ref, ub0_ref, ua1_ref, ub1_ref)

    @pl.when(parity == 1)
    def _():
        step(ua1_ref, ub1_ref, ua0_ref, ub0_ref)


def _ffn_up(h, w_up_bf16, conv_w, conv_b, tm=1024, tf=512):
    bsz, s, d = h.shape
    f = w_up_bf16.shape[1] // 2
    nf = f // tf
    nt = s // tm
    hb = tm // FFN_HALO
    npairs = bsz * nt * nf
    cb = conv_b.reshape(1, 2 * f)

    def pair(step):
        return lax.div(step, nt * nf), lax.rem(lax.div(step, nf), nt), lax.rem(step, nf)

    cur = lambda st: pair(jnp.minimum(st, npairs - 1))
    prev = lambda st: pair(jnp.maximum(st - 1, 0))
    raw = pltpu.VMEM((FFN_HALO + tm, tf), F32)
    return pl.pallas_call(
        functools.partial(_ffn_up_kernel, nt=nt, nf=nf),
        grid=(npairs + 1,),
        in_specs=[pl.BlockSpec((1, tm, d), lambda st: (cur(st)[0], cur(st)[1], 0)),
                  pl.BlockSpec((1, FFN_HALO, d),
                               lambda st: (cur(st)[0], jnp.maximum(cur(st)[1] * hb - 1, 0), 0)),
                  pl.BlockSpec((d, tf), lambda st: (0, cur(st)[2])),
                  pl.BlockSpec((d, tf), lambda st: (0, nf + cur(st)[2])),
                  pl.BlockSpec((3, tf), lambda st: (0, prev(st)[2])),
                  pl.BlockSpec((3, tf), lambda st: (0, nf + prev(st)[2])),
                  pl.BlockSpec((1, tf), lambda st: (0, prev(st)[2])),
                  pl.BlockSpec((1, tf), lambda st: (0, nf + prev(st)[2]))],
        out_specs=pl.BlockSpec((1, tm, tf), lambda st: prev(st)),
        out_shape=jax.ShapeDtypeStruct((bsz, s, f), BF16),
        scratch_shapes=[pltpu.VMEM((FFN_HALO + tm, d), BF16), raw, raw, raw, raw],
        compiler_params=_cparams(("arbitrary",)),
        name="ffn_up",
    )(h, h, w_up_bf16, w_up_bf16, conv_w, conv_w, cb, cb)


def _ffn_down_kernel(act_ref, x_ref, gt_ref, g_ref, wd_ref, o_ref):
    f = jnp.dot(act_ref[0], wd_ref[...], preferred_element_type=F32)
    o_ref[0] = x_ref[0] + gt_ref[0] * _rms(f, g_ref[...])


def _ffn_down(act, x, gt, g_post, w_down_bf16, tm=256):
    bsz, s, d = x.shape
    f = act.shape[-1]
    return pl.pallas_call(
        _ffn_down_kernel,
        grid=(bsz, s // tm),
        in_specs=[pl.BlockSpec((1, tm, f), lambda b, i: (b, i, 0)),
                  pl.BlockSpec((1, tm, d), lambda b, i: (b, i, 0)),
                  pl.BlockSpec((1, 1, d), lambda b, i: (b, 0, 0)),
                  _const_spec((1, d)), _const_spec((f, d))],
        out_specs=pl.BlockSpec((1, tm, d), lambda b, i: (b, i, 0)),
        out_shape=jax.ShapeDtypeStruct((bsz, s, d), F32),
        compiler_params=_cparams(("arbitrary", "arbitrary")),
        name="ffn_down",
    )(act, x, gt, g_post.reshape(1, d), w_down_bf16)


def kernel(x, c, w_ada, b_ada, g_pre_mix, g_post_mix, w_in, ssm_log_dt, ssm_a_re, ssm_a_im, ssm_b_re, ssm_b_im, ssm_c_re, ssm_c_im, ssm_d, ssm_w_glu, ssm_b_glu, sgu_ln_g, sgu_ln_b, sgu_w, sgu_b, g_out_ssm, g_out_sgu, w_out, g_pre_ffn, g_post_ffn, w_up, conv_w, conv_b, w_down):
    depth = w_ada.shape[0]
    d = x.shape[-1]
    d_ssm = ssm_d.shape[-1]
    heads = sgu_w.shape[1]
    hd = (w_out.shape[1] - d_ssm) // heads
    for l in range(depth):
        mod = _ada(c, w_ada[l], b_ada[l])
        sh1, sc1, gt1, sh2, sc2, gt2 = [mod[:, None, k * d:(k + 1) * d] for k in range(6)]

        zs, zuv = _inproj(x, sc1, sh1, g_pre_mix[l], w_in[l].astype(BF16), d_ssm)
        tables = _ssm_tables(ssm_log_dt[l], ssm_a_re[l], ssm_a_im[l], ssm_b_re[l], ssm_b_im[l],
                             ssm_c_re[l], ssm_c_im[l], ssm_d[l], ssm_w_glu[l])
        y_ssm = _ssm(zs, tables, ssm_b_glu[l], g_out_ssm[l])
        w_s = jnp.tril(sgu_w[l]).astype(BF16)
        b_full = jnp.repeat(sgu_b[l].T, hd, axis=1)
        y_sgu = _sgu(zuv, sgu_ln_g[l], sgu_ln_b[l], w_s, b_full, g_out_sgu[l], heads)
        wo = w_out[l].astype(BF16)
        x, h = _outproj(y_ssm, y_sgu, x, gt1, g_post_mix[l], wo[:d_ssm], wo[d_ssm:],
                        sc2, sh2, g_pre_ffn[l])

        act = _ffn_up(h, w_up[l].astype(BF16), conv_w[l], conv_b[l])
        x = _ffn_down(act, x, gt2, g_post_ffn[l], w_down[l].astype(BF16))
    return x
```

```python
import functools
import math

import jax
import jax.numpy as jnp
import numpy as np
from jax import lax
from jax.experimental import pallas as pl
from jax.experimental.pallas import tpu as pltpu

F32 = jnp.float32
BF16 = jnp.bfloat16
EPS = 1e-6

LANES = 128
SSM_L = 16
SGU_CHUNK = 128
VMEM_LIMIT = 60 * 1024 * 1024


def _cparams(sem):
    return pltpu.CompilerParams(dimension_semantics=sem, vmem_limit_bytes=VMEM_LIMIT)


def _const_spec(shape):
    nd = len(shape)
    return pl.BlockSpec(shape, lambda *_: (0,) * nd, pipeline_mode=pl.Buffered(1))


def _rms(xf, g):
    return xf * lax.rsqrt(jnp.mean(xf * xf, axis=-1, keepdims=True) + EPS) * g


def _ada_kernel(c_ref, w_ref, b_ref, o_ref):
    c = c_ref[...]
    ca = (c * jax.nn.sigmoid(c)).astype(BF16)
    o_ref[...] = jnp.dot(ca, w_ref[...].astype(BF16), preferred_element_type=F32) + b_ref[...]


def _ada(c, w, b, tn=1024):
    bsz, d = c.shape
    n = w.shape[1]
    return pl.pallas_call(
        _ada_kernel,
        grid=(n // tn,),
        in_specs=[pl.BlockSpec((bsz, d), lambda j: (0, 0)),
                  pl.BlockSpec((d, tn), lambda j: (0, j)),
                  pl.BlockSpec((1, tn), lambda j: (0, j))],
        out_specs=pl.BlockSpec((bsz, tn), lambda j: (0, j)),
        out_shape=jax.ShapeDtypeStruct((bsz, n), F32),
        compiler_params=_cparams(("arbitrary",)),
        name="ada",
    )(c, w, b.reshape(1, n))


def _inproj_kernel(x_ref, sc_ref, sh_ref, g_ref, w_ref, *rest, d_ssm, n_cast):
    cast_in = rest[:n_cast]
    zs_ref, zuv_ref = rest[n_cast:n_cast + 2]
    cast_out = rest[n_cast + 2:]
    h = _rms(x_ref[0], g_ref[...]) * (1.0 + sc_ref[0]) + sh_ref[0]
    z = jnp.dot(h.astype(BF16), w_ref[...], preferred_element_type=F32)
    for j in range(d_ssm // LANES):
        zs_ref[0, j] = z[:, j * LANES:(j + 1) * LANES]
    zuv_ref[0] = z[:, d_ssm:].astype(zuv_ref.dtype)
    for src, dst in zip(cast_in, cast_out):
        dst[...] = src[...].astype(dst.dtype)


def _inproj(x, sc, sh, g, w_bf16, d_ssm, later_weights, tm=512):
    bsz, s, d = x.shape
    d_in = w_bf16.shape[1]
    nslab = d_ssm // LANES
    nt = s // tm
    steps = bsz * nt
    wspecs = [pl.BlockSpec((w.shape[0] // steps, w.shape[1]), lambda b, i: (b * nt + i, 0))
              for w in later_weights]
    return pl.pallas_call(
        functools.partial(_inproj_kernel, d_ssm=d_ssm, n_cast=len(later_weights)),
        grid=(bsz, nt),
        in_specs=[pl.BlockSpec((1, tm, d), lambda b, i: (b, i, 0)),
                  pl.BlockSpec((1, 1, d), lambda b, i: (b, 0, 0)),
                  pl.BlockSpec((1, 1, d), lambda b, i: (b, 0, 0)),
                  _const_spec((1, d)),
                  _const_spec((d, d_in))] + wspecs,
        out_specs=[pl.BlockSpec((1, nslab, tm, LANES), lambda b, i: (b, 0, i, 0)),
                   pl.BlockSpec((1, tm, d_in - d_ssm), lambda b, i: (b, i, 0))] + wspecs,
        out_shape=[jax.ShapeDtypeStruct((bsz, nslab, s, LANES), F32),
                   jax.ShapeDtypeStruct((bsz, s, d_in - d_ssm), BF16)]
                  + [jax.ShapeDtypeStruct(w.shape, BF16) for w in later_weights],
        compiler_params=_cparams(("arbitrary", "arbitrary")),
        name="inproj",
    )(x, sc, sh, g.reshape(1, d), w_bf16, *later_weights)


def _toeplitz_kernel(k_ref, m_ref):
    gb, c, w = k_ref.shape
    lane = lax.broadcasted_iota(jnp.int32, (c, w), 1)
    for g in range(gb):
        kc = k_ref[g]
        for s in range(w // c):
            blk = kc if s == 0 else jnp.where(lane >= s * c, pltpu.roll(kc, s * c, 1), 0.0)
            m_ref[g, s * c:(s + 1) * c, :] = blk.astype(m_ref.dtype)


def _toeplitz(kcat, gb=8):
    g, c, w = kcat.shape
    return pl.pallas_call(
        _toeplitz_kernel,
        grid=(g // gb,),
        in_specs=[pl.BlockSpec((gb, c, w), lambda i: (i, 0, 0))],
        out_specs=pl.BlockSpec((gb, w, w), lambda i: (i, 0, 0)),
        out_shape=jax.ShapeDtypeStruct((g, w, w), BF16),
        compiler_params=_cparams(("arbitrary",)),
        name="toeplitz",
    )(kcat)


def _ssm_tables(log_dt, a_re, a_im, b_re, b_im, c_re, c_im, d, w_glu):
    hp = lax.Precision.HIGHEST
    L = SSM_L
    G, N, C = b_re.shape
    dt = jnp.exp(log_dt)[:, None]
    ar_dt = a_re * dt
    ai_dt = a_im * dt
    mag = jnp.exp(ar_dt)
    lb_re = mag * jnp.cos(ai_dt)
    lb_im = mag * jnp.sin(ai_dt)
    den = a_re * a_re + a_im * a_im
    nr = lb_re - 1.0
    ni = lb_im
    f_re = ((nr * a_re + ni * a_im) / den)[:, :, None]
    f_im = ((ni * a_re - nr * a_im) / den)[:, :, None]
    bb_re = f_re * b_re - f_im * b_im
    bb_im = f_re * b_im + f_im * b_re
    tau = jnp.arange(L + 1, dtype=F32)[None, :, None]
    pmag = jnp.exp(ar_dt[:, None, :] * tau)
    lp_re = pmag * jnp.cos(ai_dt[:, None, :] * tau)
    lp_im = pmag * jnp.sin(ai_dt[:, None, :] * tau)
    lq_re = jnp.repeat(lp_re[:, L - 1::-1], C, axis=1)
    lq_im = jnp.repeat(lp_im[:, L - 1::-1], C, axis=1)
    bt_re = jnp.tile(bb_re.transpose(0, 2, 1), (1, L, 1))
    bt_im = jnp.tile(bb_im.transpose(0, 2, 1), (1, L, 1))
    p = jnp.concatenate([lq_re * bt_re - lq_im * bt_im, lq_re * bt_im + lq_im * bt_re], axis=-1)
    ct_re = jnp.tile(c_re.transpose(0, 2, 1), (1, 1, L + 1))
    ct_im = jnp.tile(c_im.transpose(0, 2, 1), (1, 1, L + 1))
    lt_re = jnp.repeat(lp_re.transpose(0, 2, 1), C, axis=2)
    lt_im = jnp.repeat(lp_im.transpose(0, 2, 1), C, axis=2)
    v_re = ct_re * lt_re - ct_im * lt_im
    v_im = ct_re * lt_im + ct_im * lt_re
    q = jnp.concatenate([v_re[:, :, C:], -v_im[:, :, C:]], axis=1)
    kcat = (jnp.einsum("gni,gnx->gix", bb_re, v_re[:, :, :L * C], precision=hp)
            - jnp.einsum("gni,gnx->gix", bb_im, v_im[:, :, :L * C], precision=hp))
    skip = d.reshape(G, C)[:, :, None] * jnp.eye(C, dtype=F32)
    kcat = kcat + jnp.pad(skip, ((0, 0), (0, 0), (0, (L - 1) * C)))
    m = _toeplitz(kcat)
    l_re = lp_re[:, L]
    l_im = lp_im[:, L]
    a1 = jnp.concatenate([l_re, l_re], axis=-1)
    a2u = jnp.concatenate([-l_im, l_im], axis=-1)
    a2v = jnp.concatenate([l_im, -l_im], axis=-1)
    gps = 2 * LANES // C
    wg = w_glu.reshape(G // gps, gps, C, C)
    wg = (wg[:, :, :, None, :] * jnp.eye(gps, dtype=F32)[None, :, None, :, None]).reshape(
        G // gps, gps * C, gps * C)
    return m, p.astype(BF16), q.astype(BF16), a1, a2u, a2v, wg.astype(BF16)


def _block_swap_perm():
    idx = np.arange(8 * LANES)
    a, b, c = idx // LANES, (idx // SSM_L) % 8, idx % SSM_L
    return jnp.asarray(idx[None, :] == (b * LANES + a * SSM_L + c)[:, None], dtype=BF16)


def _ssm_kernel(zs_ref, perm_ref, m_ref, p_ref, q_ref, a1_ref, a2u_ref, a2v_ref, wg_ref, bg_ref, go_ref,
                o_ref, zf_ref, xu_ref, xv_ref, hp_ref, ys_ref, hu_ref, hv_ref, *, n, pitch, groups):
    L = SSM_L
    d = groups * L
    nslab = d // LANES

    @pl.when(pl.program_id(1) == 0)
    def _():
        hu_ref[...] = jnp.zeros_like(hu_ref)
        hv_ref[...] = jnp.zeros_like(hv_ref)

    def swap_blocks(pieces):
        lhs = jnp.concatenate([jnp.concatenate(ps, axis=1) for ps in pieces], axis=0)
        return jnp.dot(lhs, perm_ref[...], preferred_element_type=F32).astype(BF16)

    for hh in range(2):
        out = swap_blocks([[zs_ref[0, j, pl.ds(8 * hh + tp, n, stride=L), :].astype(BF16)
                            for tp in range(8)] for j in range(nslab)])
        for j in range(nslab):
            for qq in range(8):
                zf_ref[8 * j + qq, :, hh * LANES:(hh + 1) * LANES] = (
                    out[j * n:(j + 1) * n, qq * LANES:(qq + 1) * LANES])

    gx = 8

    def x_body(i, carry):
        for gi in range(gx):
            g = i * gx + gi
            xo = jnp.dot(zf_ref[g], p_ref[g], preferred_element_type=F32)
            r = pl.ds(pl.multiple_of(g * pitch, 8), n)
            xu_ref[r, :] = xo
            xv_ref[r, :] = pltpu.roll(xo, LANES // 2, 1)
        return carry

    lax.fori_loop(0, groups // gx, x_body, 0)

    a1 = a1_ref[...]
    a2u = a2u_ref[...]
    a2v = a2v_ref[...]
    ks = 8

    def scan_body(i, carry):
        hu, hv = carry
        for s in range(ks):
            kk = i * ks + s
            u = xu_ref[pl.ds(kk, groups, stride=pitch), :]
            v = xv_ref[pl.ds(kk, groups, stride=pitch), :]
            hp_ref[pl.ds(kk, groups, stride=pitch), :] = hu
            hu, hv = a1 * hu + a2u * hv + u, a1 * hv + a2v * hu + v
        return hu, hv

    hu, hv = lax.fori_loop(0, n // ks, scan_body, (hu_ref[...], hv_ref[...]))
    hu_ref[...] = hu
    hv_ref[...] = hv

    gy = 8

    def y_body(i, carry):
        for gi in range(gy):
            g = i * gy + gi
            hprev = hp_ref[pl.ds(pl.multiple_of(g * pitch, 8), n), :].astype(BF16)
            y = (jnp.dot(zf_ref[g], m_ref[g], preferred_element_type=F32)
                 + jnp.dot(hprev, q_ref[g], preferred_element_type=F32))
            zf_ref[g] = jax.nn.gelu(y).astype(BF16)
        return carry

    lax.fori_loop(0, groups // gy, y_body, 0)

    for hh in range(2):
        out = swap_blocks([[zf_ref[8 * j + qq, :, hh * LANES:(hh + 1) * LANES] for qq in range(8)]
                           for j in range(nslab)])
        for j in range(nslab):
            for tp in range(8):
                ys_ref[:, (8 * hh + tp) * d + j * LANES:(8 * hh + tp) * d + (j + 1) * LANES] = (
                    out[j * n:(j + 1) * n, tp * LANES:(tp + 1) * LANES])

    w2 = 2 * LANES
    for t in range(L):
        yb = ys_ref[:, t * d:(t + 1) * d]
        gate = jnp.concatenate(
            [jnp.dot(yb[:, i * w2:(i + 1) * w2], wg_ref[i], preferred_element_type=F32)
             for i in range(d // w2)], axis=1) + bg_ref[...]
        y = _rms(yb.astype(F32) * jax.nn.sigmoid(gate), go_ref[...])
        for j in range(nslab):
            o_ref[0, j, pl.ds(t, n, stride=L), :] = y[:, j * LANES:(j + 1) * LANES]


def _ssm(zs, tables, b_glu, g_out, sb=1024):
    m, p, q, a1, a2u, a2v, wg = tables
    bsz, nslab, s, _ = zs.shape
    d_ssm = nslab * LANES
    groups = m.shape[0]
    L = SSM_L
    n = sb // L
    pitch = n + 8
    perm = _block_swap_perm()
    kern = functools.partial(_ssm_kernel, n=n, pitch=pitch, groups=groups)
    slabs = pl.BlockSpec((1, nslab, sb, LANES), lambda b, i: (b, 0, i, 0))
    return pl.pallas_call(
        kern,
        grid=(bsz, s // sb),
        in_specs=[slabs,
                  _const_spec(perm.shape),
                  _const_spec(m.shape), _const_spec(p.shape), _const_spec(q.shape),
                  _const_spec(a1.shape), _const_spec(a2u.shape), _const_spec(a2v.shape),
                  _const_spec(wg.shape), _const_spec((1, d_ssm)), _const_spec((1, d_ssm))],
        out_specs=slabs,
        out_shape=jax.ShapeDtypeStruct((bsz, nslab, s, LANES), F32),
        scratch_shapes=[pltpu.VMEM((groups, n, 2 * LANES), BF16),
                        pltpu.VMEM((groups * pitch, LANES), F32),
                        pltpu.VMEM((groups * pitch, LANES), F32),
                        pltpu.VMEM((groups * pitch, LANES), F32),
                        pltpu.VMEM((n, L * d_ssm), BF16),
                        pltpu.VMEM((groups, LANES), F32),
                        pltpu.VMEM((groups, LANES), F32)],
        compiler_params=_cparams(("arbitrary", "arbitrary")),
        name="ssm",
    )(zs, perm, m, p, q, a1, a2u, a2v, wg, b_glu.reshape(1, d_ssm), g_out.reshape(1, d_ssm))


def _sgu_kernel(zuv_ref, lng_ref, lnb_ref, w_ref, bs_ref, go_ref, o_ref, *, heads, hd):
    d = heads * hd
    ts = zuv_ref.shape[1]
    nc = ts // SGU_CHUNK
    u = jax.nn.gelu(zuv_ref[0, :, :d].astype(F32))
    v = jax.nn.gelu(zuv_ref[0, :, d:].astype(F32))
    mu = jnp.mean(v, axis=-1, keepdims=True)
    vc = v - mu
    v = vc * lax.rsqrt(jnp.mean(vc * vc, axis=-1, keepdims=True) + EPS) * lng_ref[...] + lnb_ref[...]
    vb = v.astype(BF16)
    cols = []
    for h in range(heads):
        vh = jnp.concatenate([vb[c * SGU_CHUNK:(c + 1) * SGU_CHUNK, h * hd:(h + 1) * hd]
                              for c in range(nc)], axis=1)
        mixed = jnp.dot(w_ref[h], vh, preferred_element_type=F32)
        mixed = jnp.concatenate([mixed[:, c * hd:(c + 1) * hd] for c in range(nc)], axis=0)
        cols.append(mixed)
    mixed = jnp.concatenate(cols, axis=1)
    bias = jnp.concatenate([bs_ref[...]] * nc, axis=0)
    y = u * (mixed + bias)
    o_ref[0] = _rms(y, go_ref[...]).astype(o_ref.dtype)


def _sgu(zuv, ln_g, ln_b, w_masked_bf16, b_full, g_out, heads, ts=512):
    bsz, s, d2 = zuv.shape
    d = d2 // 2
    hd = d // heads
    return pl.pallas_call(
        functools.partial(_sgu_kernel, heads=heads, hd=hd),
        grid=(bsz, s // ts),
        in_specs=[pl.BlockSpec((1, ts, d2), lambda b, i: (b, i, 0)),
                  _const_spec((1, d)), _const_spec((1, d)),
                  _const_spec(w_masked_bf16.shape), _const_spec(b_full.shape), _const_spec((1, d))],
        out_specs=pl.BlockSpec((1, ts, d), lambda b, i: (b, i, 0)),
        out_shape=jax.ShapeDtypeStruct((bsz, s, d), BF16),
        compiler_params=_cparams(("arbitrary", "arbitrary")),
        name="sgu",
    )(zuv, ln_g.reshape(1, d), ln_b.reshape(1, d), w_masked_bf16, b_full, g_out.reshape(1, d))


def _outproj_kernel(ya_ref, yb_ref, x_ref, gt_ref, g_ref, wa_ref, wb_ref, sc_ref, sh_ref, gn_ref,
                    o_ref, h_ref):
    ya = jnp.concatenate([ya_ref[0, j] for j in range(ya_ref.shape[1])], axis=1).astype(BF16)
    y = (jnp.dot(ya, wa_ref[...], preferred_element_type=F32)
         + jnp.dot(yb_ref[0], wb_ref[...], preferred_element_type=F32))
    x1 = x_ref[0] + gt_ref[0] * _rms(y, g_ref[...])
    o_ref[0] = x1
    h_ref[0] = (_rms(x1, gn_ref[...]) * (1.0 + sc_ref[0]) + sh_ref[0]).astype(h_ref.dtype)


def _outproj(ya, yb, x, gt, g, w_bf16, sc_next, sh_next, g_next, tm=512):
    bsz, s, d = x.shape
    nslab = ya.shape[1]
    db = yb.shape[-1]
    assert nslab * LANES == db and w_bf16.shape[0] == 2 * db
    row = lambda w: pl.BlockSpec((1, tm, w), lambda b, i: (b, i, 0))
    per_batch = pl.BlockSpec((1, 1, d), lambda b, i: (b, 0, 0))
    half = lambda k: pl.BlockSpec((db, d), lambda b, i: (k, 0), pipeline_mode=pl.Buffered(1))
    return pl.pallas_call(
        _outproj_kernel,
        grid=(bsz, s // tm),
        in_specs=[pl.BlockSpec((1, nslab, tm, LANES), lambda b, i: (b, 0, i, 0)), row(db), row(d),
                  per_batch, _const_spec((1, d)), half(0), half(1),
                  per_batch, per_batch, _const_spec((1, d))],
        out_specs=[row(d), row(d)],
        out_shape=[jax.ShapeDtypeStruct((bsz, s, d), F32), jax.ShapeDtypeStruct((bsz, s, d), BF16)],
        compiler_params=_cparams(("arbitrary", "arbitrary")),
        name="outproj",
    )(ya, yb, x, gt, g.reshape(1, d), w_bf16, w_bf16, sc_next, sh_next, g_next.reshape(1, d))


FFN_HALO = 16

def _ffn_up_kernel(hin_ref, hh_ref, wa_ref, wb_ref, cwa_ref, cwb_ref, cba_ref, cbb_ref, o_ref, h_ref):
    i = pl.program_id(1)
    j = pl.program_id(2)

    @pl.when(j == 0)
    def _():
        halo = hh_ref[0]
        h_ref[:FFN_HALO, :] = jnp.where(i > 0, halo, jnp.zeros_like(halo))
        h_ref[FFN_HALO:, :] = hin_ref[0]

    h = h_ref[...]

    def branch(w_ref, cw_ref, cb_ref):
        up = jnp.dot(h, w_ref[...], preferred_element_type=F32)
        cw = cw_ref[...]
        r = (cw[0:1] * pltpu.roll(up, 2, 0) + cw[1:2] * pltpu.roll(up, 1, 0) + cw[2:3] * up)
        return r[FFN_HALO:] + cb_ref[...]

    a = branch(wa_ref, cwa_ref, cba_ref)
    b = branch(wb_ref, cwb_ref, cbb_ref)
    o_ref[0] = (a * jax.nn.sigmoid(a) * b).astype(o_ref.dtype)


def _ffn_up(h, w_up_bf16, conv_w, conv_b, tm=1024, tf=512):
    bsz, s, d = h.shape
    f = w_up_bf16.shape[1] // 2
    nf = f // tf
    hb = tm // FFN_HALO
    cb = conv_b.reshape(1, 2 * f)
    return pl.pallas_call(
        _ffn_up_kernel,
        grid=(bsz, s // tm, nf),
        in_specs=[pl.BlockSpec((1, tm, d), lambda b, i, j: (b, i, 0)),
                  pl.BlockSpec((1, FFN_HALO, d), lambda b, i, j: (b, jnp.maximum(i * hb - 1, 0), 0)),
                  pl.BlockSpec((d, tf), lambda b, i, j: (0, j)),
                  pl.BlockSpec((d, tf), lambda b, i, j: (0, nf + j)),
                  pl.BlockSpec((3, tf), lambda b, i, j: (0, j)),
                  pl.BlockSpec((3, tf), lambda b, i, j: (0, nf + j)),
                  pl.BlockSpec((1, tf), lambda b, i, j: (0, j)),
                  pl.BlockSpec((1, tf), lambda b, i, j: (0, nf + j))],
        out_specs=pl.BlockSpec((1, tm, tf), lambda b, i, j: (b, i, j)),
        out_shape=jax.ShapeDtypeStruct((bsz, s, f), BF16),
        scratch_shapes=[pltpu.VMEM((FFN_HALO + tm, d), BF16)],
        compiler_params=_cparams(("arbitrary", "arbitrary", "arbitrary")),
        name="ffn_up",
    )(h, h, w_up_bf16, w_up_bf16, conv_w, conv_w, cb, cb)


def _ffn_down_kernel(act_ref, x_ref, gt_ref, g_ref, wd_ref, o_ref):
    f = jnp.dot(act_ref[0], wd_ref[...], preferred_element_type=F32)
    o_ref[0] = x_ref[0] + gt_ref[0] * _rms(f, g_ref[...])


def _ffn_down(act, x, gt, g_post, w_down_bf16, tm=256):
    bsz, s, d = x.shape
    f = act.shape[-1]
    return pl.pallas_call(
        _ffn_down_kernel,
        grid=(bsz, s // tm),
        in_specs=[pl.BlockSpec((1, tm, f), lambda b, i: (b, i, 0)),
                  pl.BlockSpec((1, tm, d), lambda b, i: (b, i, 0)),
                  pl.BlockSpec((1, 1, d), lambda b, i: (b, 0, 0)),
                  _const_spec((1, d)), _const_spec((f, d))],
        out_specs=pl.BlockSpec((1, tm, d), lambda b, i: (b, i, 0)),
        out_shape=jax.ShapeDtypeStruct((bsz, s, d), F32),
        compiler_params=_cparams(("arbitrary", "arbitrary")),
        name="ffn_down",
    )(act, x, gt, g_post.reshape(1, d), w_down_bf16)


def kernel(x, c, w_ada, b_ada, g_pre_mix, g_post_mix, w_in, ssm_log_dt, ssm_a_re, ssm_a_im, ssm_b_re, ssm_b_im, ssm_c_re, ssm_c_im, ssm_d, ssm_w_glu, ssm_b_glu, sgu_ln_g, sgu_ln_b, sgu_w, sgu_b, g_out_ssm, g_out_sgu, w_out, g_pre_ffn, g_post_ffn, w_up, conv_w, conv_b, w_down):
    depth = w_ada.shape[0]
    d = x.shape[-1]
    d_ssm = ssm_d.shape[-1]
    heads = sgu_w.shape[1]
    hd = (w_out.shape[1] - d_ssm) // heads
    for l in range(depth):
        mod = _ada(c, w_ada[l], b_ada[l])
        sh1, sc1, gt1, sh2, sc2, gt2 = [mod[:, None, k * d:(k + 1) * d] for k in range(6)]

        zs, zuv, wo, wu, wd = _inproj(x, sc1, sh1, g_pre_mix[l], w_in[l].astype(BF16), d_ssm,
                                      (w_out[l], w_up[l], w_down[l]))
        tables = _ssm_tables(ssm_log_dt[l], ssm_a_re[l], ssm_a_im[l], ssm_b_re[l], ssm_b_im[l],
                             ssm_c_re[l], ssm_c_im[l], ssm_d[l], ssm_w_glu[l])
        y_ssm = _ssm(zs, tables, ssm_b_glu[l], g_out_ssm[l])
        w_s = jnp.tril(sgu_w[l]).astype(BF16)
        b_full = jnp.repeat(sgu_b[l].T, hd, axis=1)
        y_sgu = _sgu(zuv, sgu_ln_g[l], sgu_ln_b[l], w_s, b_full, g_out_sgu[l], heads)
        x, h = _outproj(y_ssm, y_sgu, x, gt1, g_post_mix[l], wo, sc2, sh2, g_pre_ffn[l])

        act = _ffn_up(h, wu, conv_w[l], conv_b[l])
        x = _ffn_down(act, x, gt2, g_post_ffn[l], wd)
    return x
```

```python
import functools
import math

import jax
import jax.numpy as jnp
import numpy as np
from jax import lax
from jax.experimental import pallas as pl
from jax.experimental.pallas import tpu as pltpu

F32 = jnp.float32
BF16 = jnp.bfloat16
EPS = 1e-6

LANES = 128
SSM_L = 16
SGU_CHUNK = 128
VMEM_LIMIT = 60 * 1024 * 1024


def _cparams(sem):
    return pltpu.CompilerParams(dimension_semantics=sem, vmem_limit_bytes=VMEM_LIMIT)


def _const_spec(shape):
    nd = len(shape)
    return pl.BlockSpec(shape, lambda *_: (0,) * nd, pipeline_mode=pl.Buffered(1))


def _rms(xf, g):
    return xf * lax.rsqrt(jnp.mean(xf * xf, axis=-1, keepdims=True) + EPS) * g


def _ada_kernel(c_ref, w_ref, b_ref, o_ref):
    c = c_ref[...]
    ca = (c * jax.nn.sigmoid(c)).astype(BF16)
    o_ref[...] = jnp.dot(ca, w_ref[...].astype(BF16), preferred_element_type=F32) + b_ref[...]


def _ada(c, w, b, tn=1024):
    bsz, d = c.shape
    n = w.shape[1]
    return pl.pallas_call(
        _ada_kernel,
        grid=(n // tn,),
        in_specs=[pl.BlockSpec((bsz, d), lambda j: (0, 0)),
                  pl.BlockSpec((d, tn), lambda j: (0, j)),
                  pl.BlockSpec((1, tn), lambda j: (0, j))],
        out_specs=pl.BlockSpec((bsz, tn), lambda j: (0, j)),
        out_shape=jax.ShapeDtypeStruct((bsz, n), F32),
        compiler_params=_cparams(("arbitrary",)),
        name="ada",
    )(c, w, b.reshape(1, n))


def _inproj_kernel(x_ref, sc_ref, sh_ref, g_ref, w_ref, *rest, d_ssm, n_cast):
    cast_in = rest[:n_cast]
    zs_ref, zuv_ref = rest[n_cast:n_cast + 2]
    cast_out = rest[n_cast + 2:]
    h = _rms(x_ref[0], g_ref[...]) * (1.0 + sc_ref[0]) + sh_ref[0]
    z = jnp.dot(h.astype(BF16), w_ref[...], preferred_element_type=F32)
    for j in range(d_ssm // LANES):
        zs_ref[0, j] = z[:, j * LANES:(j + 1) * LANES]
    zuv_ref[0] = z[:, d_ssm:].astype(zuv_ref.dtype)
    for src, dst in zip(cast_in, cast_out):
        dst[...] = src[...].astype(dst.dtype)


def _inproj(x, sc, sh, g, w_bf16, d_ssm, later_weights, tm=512):
    bsz, s, d = x.shape
    d_in = w_bf16.shape[1]
    nslab = d_ssm // LANES
    nt = s // tm
    steps = bsz * nt
    wspecs = [pl.BlockSpec((w.shape[0] // steps, w.shape[1]), lambda b, i: (b * nt + i, 0))
              for w in later_weights]
    return pl.pallas_call(
        functools.partial(_inproj_kernel, d_ssm=d_ssm, n_cast=len(later_weights)),
        grid=(bsz, nt),
        in_specs=[pl.BlockSpec((1, tm, d), lambda b, i: (b, i, 0)),
                  pl.BlockSpec((1, 1, d), lambda b, i: (b, 0, 0)),
                  pl.BlockSpec((1, 1, d), lambda b, i: (b, 0, 0)),
                  _const_spec((1, d)),
                  _const_spec((d, d_in))] + wspecs,
        out_specs=[pl.BlockSpec((1, nslab, tm, LANES), lambda b, i: (b, 0, i, 0)),
                   pl.BlockSpec((1, tm, d_in - d_ssm), lambda b, i: (b, i, 0))] + wspecs,
        out_shape=[jax.ShapeDtypeStruct((bsz, nslab, s, LANES), F32),
                   jax.ShapeDtypeStruct((bsz, s, d_in - d_ssm), BF16)]
                  + [jax.ShapeDtypeStruct(w.shape, BF16) for w in later_weights],
        compiler_params=_cparams(("arbitrary", "arbitrary")),
        name="inproj",
    )(x, sc, sh, g.reshape(1, d), w_bf16, *later_weights)


def _toeplitz_kernel(k_ref, m_ref):
    gb, c, w = k_ref.shape
    lane = lax.broadcasted_iota(jnp.int32, (c, w), 1)
    for g in range(gb):
        kc = k_ref[g]
        for s in range(w // c):
            blk = kc if s == 0 else jnp.where(lane >= s * c, pltpu.roll(kc, s * c, 1), 0.0)
            m_ref[g, s * c:(s + 1) * c, :] = blk.astype(m_ref.dtype)


def _toeplitz(kcat, gb=8):
    g, c, w = kcat.shape
    return pl.pallas_call(
        _toeplitz_kernel,
        grid=(g // gb,),
        in_specs=[pl.BlockSpec((gb, c, w), lambda i: (i, 0, 0))],
        out_specs=pl.BlockSpec((gb, w, w), lambda i: (i, 0, 0)),
        out_shape=jax.ShapeDtypeStruct((g, w, w), BF16),
        compiler_params=_cparams(("arbitrary",)),
        name="toeplitz",
    )(kcat)


def _ssm_tables(log_dt, a_re, a_im, b_re, b_im, c_re, c_im, d, w_glu):
    hp = lax.Precision.HIGHEST
    L = SSM_L
    G, N, C = b_re.shape
    dt = jnp.exp(log_dt)[:, None]
    ar_dt = a_re * dt
    ai_dt = a_im * dt
    mag = jnp.exp(ar_dt)
    lb_re = mag * jnp.cos(ai_dt)
    lb_im = mag * jnp.sin(ai_dt)
    den = a_re * a_re + a_im * a_im
    nr = lb_re - 1.0
    ni = lb_im
    f_re = ((nr * a_re + ni * a_im) / den)[:, :, None]
    f_im = ((ni * a_re - nr * a_im) / den)[:, :, None]
    bb_re = f_re * b_re - f_im * b_im
    bb_im = f_re * b_im + f_im * b_re
    tau = jnp.arange(L + 1, dtype=F32)[None, :, None]
    pmag = jnp.exp(ar_dt[:, None, :] * tau)
    lp_re = pmag * jnp.cos(ai_dt[:, None, :] * tau)
    lp_im = pmag * jnp.sin(ai_dt[:, None, :] * tau)
    lq_re = jnp.repeat(lp_re[:, L - 1::-1], C, axis=1)
    lq_im = jnp.repeat(lp_im[:, L - 1::-1], C, axis=1)
    bt_re = jnp.tile(bb_re.transpose(0, 2, 1), (1, L, 1))
    bt_im = jnp.tile(bb_im.transpose(0, 2, 1), (1, L, 1))
    p = jnp.concatenate([lq_re * bt_re - lq_im * bt_im, lq_re * bt_im + lq_im * bt_re], axis=-1)
    ct_re = jnp.tile(c_re.transpose(0, 2, 1), (1, 1, L + 1))
    ct_im = jnp.tile(c_im.transpose(0, 2, 1), (1, 1, L + 1))
    lt_re = jnp.repeat(lp_re.transpose(0, 2, 1), C, axis=2)
    lt_im = jnp.repeat(lp_im.transpose(0, 2, 1), C, axis=2)
    v_re = ct_re * lt_re - ct_im * lt_im
    v_im = ct_re * lt_im + ct_im * lt_re
    q = jnp.concatenate([v_re[:, :, C:], -v_im[:, :, C:]], axis=1)
    kcat = (jnp.einsum("gni,gnx->gix", bb_re, v_re[:, :, :L * C], precision=hp)
            - jnp.einsum("gni,gnx->gix", bb_im, v_im[:, :, :L * C], precision=hp))
    skip = d.reshape(G, C)[:, :, None] * jnp.eye(C, dtype=F32)
    kcat = kcat + jnp.pad(skip, ((0, 0), (0, 0), (0, (L - 1) * C)))
    m = _toeplitz(kcat)
    l_re = lp_re[:, L]
    l_im = lp_im[:, L]
    a1 = jnp.concatenate([l_re, l_re], axis=-1)
    a2u = jnp.concatenate([-l_im, l_im], axis=-1)
    a2v = jnp.concatenate([l_im, -l_im], axis=-1)
    gps = 2 * LANES // C
    wg = w_glu.reshape(G // gps, gps, C, C)
    wg = (wg[:, :, :, None, :] * jnp.eye(gps, dtype=F32)[None, :, None, :, None]).reshape(
        G // gps, gps * C, gps * C)
    return m, p.astype(BF16), q.astype(BF16), a1, a2u, a2v, wg.astype(BF16)


def _block_swap_perm():
    idx = np.arange(8 * LANES)
    a, b, c = idx // LANES, (idx // SSM_L) % 8, idx % SSM_L
    return jnp.asarray(idx[None, :] == (b * LANES + a * SSM_L + c)[:, None], dtype=BF16)


def _ssm_kernel(zs_ref, perm_ref, m_ref, p_ref, q_ref, a1_ref, a2u_ref, a2v_ref, wg_ref, bg_ref, go_ref,
                o_ref, zf_ref, xu_ref, xv_ref, hp_ref, ys_ref, hu_ref, hv_ref, *, n, pitch, groups):
    L = SSM_L
    d = groups * L
    nslab = d // LANES

    @pl.when(pl.program_id(1) == 0)
    def _():
        hu_ref[...] = jnp.zeros_like(hu_ref)
        hv_ref[...] = jnp.zeros_like(hv_ref)

    def swap_blocks(pieces):
        lhs = jnp.concatenate([jnp.concatenate(ps, axis=1) for ps in pieces], axis=0)
        return jnp.dot(lhs, perm_ref[...], preferred_element_type=F32).astype(BF16)

    for hh in range(2):
        out = swap_blocks([[zs_ref[0, j, pl.ds(8 * hh + tp, n, stride=L), :].astype(BF16)
                            for tp in range(8)] for j in range(nslab)])
        for j in range(nslab):
            for qq in range(8):
                zf_ref[8 * j + qq, :, hh * LANES:(hh + 1) * LANES] = (
                    out[j * n:(j + 1) * n, qq * LANES:(qq + 1) * LANES])

    gx = 8

    def x_body(i, carry):
        for gi in range(gx):
            g = i * gx + gi
            xo = jnp.dot(zf_ref[g], p_ref[g], preferred_element_type=F32)
            r = pl.ds(pl.multiple_of(g * pitch, 8), n)
            xu_ref[r, :] = xo
            xv_ref[r, :] = pltpu.roll(xo, LANES // 2, 1)
        return carry

    lax.fori_loop(0, groups // gx, x_body, 0)

    a1 = a1_ref[...]
    a2u = a2u_ref[...]
    a2v = a2v_ref[...]
    ks = 8

    def scan_body(i, carry):
        hu, hv = carry
        for s in range(ks):
            kk = i * ks + s
            u = xu_ref[pl.ds(kk, groups, stride=pitch), :]
            v = xv_ref[pl.ds(kk, groups, stride=pitch), :]
            hp_ref[pl.ds(kk, groups, stride=pitch), :] = hu
            hu, hv = a1 * hu + a2u * hv + u, a1 * hv + a2v * hu + v
        return hu, hv

    hu, hv = lax.fori_loop(0, n // ks, scan_body, (hu_ref[...], hv_ref[...]))
    hu_ref[...] = hu
    hv_ref[...] = hv

    gy = 8

    def y_body(i, carry):
        for gi in range(gy):
            g = i * gy + gi
            hprev = hp_ref[pl.ds(pl.multiple_of(g * pitch, 8), n), :].astype(BF16)
            y = (jnp.dot(zf_ref[g], m_ref[g], preferred_element_type=F32)
                 + jnp.dot(hprev, q_ref[g], preferred_element_type=F32))
            zf_ref[g] = jax.nn.gelu(y).astype(BF16)
        return carry

    lax.fori_loop(0, groups // gy, y_body, 0)

    for hh in range(2):
        out = swap_blocks([[zf_ref[8 * j + qq, :, hh * LANES:(hh + 1) * LANES] for qq in range(8)]
                           for j in range(nslab)])
        for j in range(nslab):
            for tp in range(8):
                ys_ref[:, (8 * hh + tp) * d + j * LANES:(8 * hh + tp) * d + (j + 1) * LANES] = (
                    out[j * n:(j + 1) * n, tp * LANES:(tp + 1) * LANES])

    w2 = 2 * LANES
    for t in range(L):
        yb = ys_ref[:, t * d:(t + 1) * d]
        gate = jnp.concatenate(
            [jnp.dot(yb[:, i * w2:(i + 1) * w2], wg_ref[i], preferred_element_type=F32)
             for i in range(d // w2)], axis=1) + bg_ref[...]
        y = _rms(yb.astype(F32) * jax.nn.sigmoid(gate), go_ref[...])
        for j in range(nslab):
            o_ref[0, j, pl.ds(t, n, stride=L), :] = y[:, j * LANES:(j + 1) * LANES]


def _ssm(zs, tables, b_glu, g_out, sb=1024):
    m, p, q, a1, a2u, a2v, wg = tables
    bsz, nslab, s, _ = zs.shape
    d_ssm = nslab * LANES
    groups = m.shape[0]
    L = SSM_L
    n = sb // L
    pitch = n + 8
    perm = _block_swap_perm()
    kern = functools.partial(_ssm_kernel, n=n, pitch=pitch, groups=groups)
    slabs = pl.BlockSpec((1, nslab, sb, LANES), lambda b, i: (b, 0, i, 0))
    return pl.pallas_call(
        kern,
        grid=(bsz, s // sb),
        in_specs=[slabs,
                  _const_spec(perm.shape),
                  _const_spec(m.shape), _const_spec(p.shape), _const_spec(q.shape),
                  _const_spec(a1.shape), _const_spec(a2u.shape), _const_spec(a2v.shape),
                  _const_spec(wg.shape), _const_spec((1, d_ssm)), _const_spec((1, d_ssm))],
        out_specs=slabs,
        out_shape=jax.ShapeDtypeStruct((bsz, nslab, s, LANES), F32),
        scratch_shapes=[pltpu.VMEM((groups, n, 2 * LANES), BF16),
                        pltpu.VMEM((groups * pitch, LANES), F32),
                        pltpu.VMEM((groups * pitch, LANES), F32),
                        pltpu.VMEM((groups * pitch, LANES), F32),
                        pltpu.VMEM((n, L * d_ssm), BF16),
                        pltpu.VMEM((groups, LANES), F32),
                        pltpu.VMEM((groups, LANES), F32)],
        compiler_params=_cparams(("arbitrary", "arbitrary")),
        name="ssm",
    )(zs, perm, m, p, q, a1, a2u, a2v, wg, b_glu.reshape(1, d_ssm), g_out.reshape(1, d_ssm))


def _sgu_kernel(zuv_ref, lng_ref, lnb_ref, w_ref, bs_ref, go_ref, o_ref, *, heads, hd):
    d = heads * hd
    ts = zuv_ref.shape[1]
    nc = ts // SGU_CHUNK
    u = jax.nn.gelu(zuv_ref[0, :, :d].astype(F32))
    v = jax.nn.gelu(zuv_ref[0, :, d:].astype(F32))
    mu = jnp.mean(v, axis=-1, keepdims=True)
    vc = v - mu
    v = vc * lax.rsqrt(jnp.mean(vc * vc, axis=-1, keepdims=True) + EPS) * lng_ref[...] + lnb_ref[...]
    vb = v.astype(BF16)
    cols = []
    for h in range(heads):
        vh = jnp.concatenate([vb[c * SGU_CHUNK:(c + 1) * SGU_CHUNK, h * hd:(h + 1) * hd]
                              for c in range(nc)], axis=1)
        mixed = jnp.dot(w_ref[h], vh, preferred_element_type=F32)
        mixed = jnp.concatenate([mixed[:, c * hd:(c + 1) * hd] for c in range(nc)], axis=0)
        cols.append(mixed)
    mixed = jnp.concatenate(cols, axis=1)
    bias = jnp.concatenate([bs_ref[...]] * nc, axis=0)
    y = u * (mixed + bias)
    o_ref[0] = _rms(y, go_ref[...]).astype(o_ref.dtype)


def _sgu(zuv, ln_g, ln_b, w_masked_bf16, b_full, g_out, heads, ts=512):
    bsz, s, d2 = zuv.shape
    d = d2 // 2
    hd = d // heads
    return pl.pallas_call(
        functools.partial(_sgu_kernel, heads=heads, hd=hd),
        grid=(bsz, s // ts),
        in_specs=[pl.BlockSpec((1, ts, d2), lambda b, i: (b, i, 0)),
                  _const_spec((1, d)), _const_spec((1, d)),
                  _const_spec(w_masked_bf16.shape), _const_spec(b_full.shape), _const_spec((1, d))],
        out_specs=pl.BlockSpec((1, ts, d), lambda b, i: (b, i, 0)),
        out_shape=jax.ShapeDtypeStruct((bsz, s, d), BF16),
        compiler_params=_cparams(("arbitrary", "arbitrary")),
        name="sgu",
    )(zuv, ln_g.reshape(1, d), ln_b.reshape(1, d), w_masked_bf16, b_full, g_out.reshape(1, d))


def _outproj_kernel(ya_ref, yb_ref, x_ref, gt_ref, g_ref, wa_ref, wb_ref, sc_ref, sh_ref, gn_ref,
                    o_ref, h_ref):
    ya = jnp.concatenate([ya_ref[0, j] for j in range(ya_ref.shape[1])], axis=1).astype(BF16)
    y = (jnp.dot(ya, wa_ref[...], preferred_element_type=F32)
         + jnp.dot(yb_ref[0], wb_ref[...], preferred_element_type=F32))
    x1 = x_ref[0] + gt_ref[0] * _rms(y, g_ref[...])
    o_ref[0] = x1
    h = _rms(x1, gn_ref[...]) * (1.0 + sc_ref[0]) + sh_ref[0]
    for c in range(h_ref.shape[1]):
        h_ref[0, c] = h[:, c * LANES:(c + 1) * LANES]


def _outproj(ya, yb, x, gt, g, w_bf16, sc_next, sh_next, g_next, tm=512):
    bsz, s, d = x.shape
    nslab = ya.shape[1]
    db = yb.shape[-1]
    assert nslab * LANES == db and w_bf16.shape[0] == 2 * db
    row = lambda w: pl.BlockSpec((1, tm, w), lambda b, i: (b, i, 0))
    per_batch = pl.BlockSpec((1, 1, d), lambda b, i: (b, 0, 0))
    half = lambda k: pl.BlockSpec((db, d), lambda b, i: (k, 0), pipeline_mode=pl.Buffered(1))
    return pl.pallas_call(
        _outproj_kernel,
        grid=(bsz, s // tm),
        in_specs=[pl.BlockSpec((1, nslab, tm, LANES), lambda b, i: (b, 0, i, 0)), row(db), row(d),
                  per_batch, _const_spec((1, d)), half(0), half(1),
                  per_batch, per_batch, _const_spec((1, d))],
        out_specs=[row(d), pl.BlockSpec((1, d // LANES, tm, LANES), lambda b, i: (b, 0, i, 0))],
        out_shape=[jax.ShapeDtypeStruct((bsz, s, d), F32),
                   jax.ShapeDtypeStruct((bsz, d // LANES, s, LANES), F32)],
        compiler_params=_cparams(("arbitrary", "arbitrary")),
        name="outproj",
    )(ya, yb, x, gt, g.reshape(1, d), w_bf16, w_bf16, sc_next, sh_next, g_next.reshape(1, d))


FFN_HALO = 16

def _ffn_up_kernel(hin_ref, hh_ref, wa_ref, wb_ref, cwa_ref, cwb_ref, cba_ref, cbb_ref, o_ref,
                   ext_ref, h_ref, act_ref):
    i = pl.program_id(1)
    j = pl.program_id(2)
    rows = h_ref.shape[0]
    nq = rows // 8
    nslab = hin_ref.shape[1]

    @pl.when(j == 0)
    def _():
        for c in range(nslab):
            halo = hh_ref[0, c]
            ext_ref[c, :FFN_HALO, :] = jnp.where(i > 0, halo, jnp.zeros_like(halo))
            ext_ref[c, FFN_HALO:, :] = hin_ref[0, c]

        def gather(m, carry):
            r0 = pl.multiple_of(m * 16, 16)
            for c in range(nslab):
                g = jnp.concatenate([ext_ref[c, pl.ds(2 * m, 8, stride=nq), :],
                                     ext_ref[c, pl.ds(2 * m + 1, 8, stride=nq), :]], axis=0)
                h_ref[pl.ds(r0, 16), c * LANES:(c + 1) * LANES] = g.astype(BF16)
            return carry

        lax.fori_loop(0, nq // 2, gather, 0)

    h = h_ref[...]

    def branch(w_ref, cw_ref, cb_ref):
        up = jnp.dot(h, w_ref[...], preferred_element_type=F32)
        cw = cw_ref[...]
        last = pltpu.roll(up[rows - 8:], 1, 0)
        last2 = pltpu.roll(up[rows - 16:rows - 8], 1, 0)
        m1 = jnp.concatenate([last, up[:rows - 8]], axis=0)
        m2 = jnp.concatenate([last2, last, up[:rows - 16]], axis=0)
        return cw[0:1] * m2 + cw[1:2] * m1 + cw[2:3] * up + cb_ref[...]

    a = branch(wa_ref, cwa_ref, cba_ref)
    b = branch(wb_ref, cwb_ref, cbb_ref)
    act = a * jax.nn.sigmoid(a) * b
    for q in range(nq):
        for c in range(act_ref.shape[0]):
            act_ref[c, pl.ds(q, 8, stride=nq), :] = act[8 * q:8 * q + 8, c * LANES:(c + 1) * LANES]
    o_ref[0] = jnp.concatenate([act_ref[c, FFN_HALO:, :] for c in range(act_ref.shape[0])],
                               axis=1).astype(o_ref.dtype)


def _ffn_up(h, w_up_bf16, conv_w, conv_b, tm=1024, tf=512):
    bsz, nslab, s, _ = h.shape
    d = nslab * LANES
    f = w_up_bf16.shape[1] // 2
    nf = f // tf
    hb = tm // FFN_HALO
    rows = FFN_HALO + tm
    assert (rows // 8) % 8 in (2, 4, 6)
    cb = conv_b.reshape(1, 2 * f)
    return pl.pallas_call(
        _ffn_up_kernel,
        grid=(bsz, s // tm, nf),
        in_specs=[pl.BlockSpec((1, nslab, tm, LANES), lambda b, i, j: (b, 0, i, 0)),
                  pl.BlockSpec((1, nslab, FFN_HALO, LANES),
                               lambda b, i, j: (b, 0, jnp.maximum(i * hb - 1, 0), 0)),
                  pl.BlockSpec((d, tf), lambda b, i, j: (0, j)),
                  pl.BlockSpec((d, tf), lambda b, i, j: (0, nf + j)),
                  pl.BlockSpec((3, tf), lambda b, i, j: (0, j)),
                  pl.BlockSpec((3, tf), lambda b, i, j: (0, nf + j)),
                  pl.BlockSpec((1, tf), lambda b, i, j: (0, j)),
                  pl.BlockSpec((1, tf), lambda b, i, j: (0, nf + j))],
        out_specs=pl.BlockSpec((1, tm, tf), lambda b, i, j: (b, i, j)),
        out_shape=jax.ShapeDtypeStruct((bsz, s, f), BF16),
        scratch_shapes=[pltpu.VMEM((nslab, rows, LANES), F32),
                        pltpu.VMEM((rows, d), BF16),
                        pltpu.VMEM((tf // LANES, rows, LANES), F32)],
        compiler_params=_cparams(("arbitrary", "arbitrary", "arbitrary")),
        name="ffn_up",
    )(h, h, w_up_bf16, w_up_bf16, conv_w, conv_w, cb, cb)


def _ffn_down_kernel(act_ref, x_ref, gt_ref, g_ref, wd_ref, o_ref):
    f = jnp.dot(act_ref[0], wd_ref[...], preferred_element_type=F32)
    o_ref[0] = x_ref[0] + gt_ref[0] * _rms(f, g_ref[...])


def _ffn_down(act, x, gt, g_post, w_down_bf16, tm=256):
    bsz, s, d = x.shape
    f = act.shape[-1]
    return pl.pallas_call(
        _ffn_down_kernel,
        grid=(bsz, s // tm),
        in_specs=[pl.BlockSpec((1, tm, f), lambda b, i: (b, i, 0)),
                  pl.BlockSpec((1, tm, d), lambda b, i: (b, i, 0)),
                  pl.BlockSpec((1, 1, d), lambda b, i: (b, 0, 0)),
                  _const_spec((1, d)), _const_spec((f, d))],
        out_specs=pl.BlockSpec((1, tm, d), lambda b, i: (b, i, 0)),
        out_shape=jax.ShapeDtypeStruct((bsz, s, d), F32),
        compiler_params=_cparams(("arbitrary", "arbitrary")),
        name="ffn_down",
    )(act, x, gt, g_post.reshape(1, d), w_down_bf16)


def kernel(x, c, w_ada, b_ada, g_pre_mix, g_post_mix, w_in, ssm_log_dt, ssm_a_re, ssm_a_im, ssm_b_re, ssm_b_im, ssm_c_re, ssm_c_im, ssm_d, ssm_w_glu, ssm_b_glu, sgu_ln_g, sgu_ln_b, sgu_w, sgu_b, g_out_ssm, g_out_sgu, w_out, g_pre_ffn, g_post_ffn, w_up, conv_w, conv_b, w_down):
    depth = w_ada.shape[0]
    d = x.shape[-1]
    d_ssm = ssm_d.shape[-1]
    heads = sgu_w.shape[1]
    hd = (w_out.shape[1] - d_ssm) // heads
    for l in range(depth):
        mod = _ada(c, w_ada[l], b_ada[l])
        sh1, sc1, gt1, sh2, sc2, gt2 = [mod[:, None, k * d:(k + 1) * d] for k in range(6)]

        zs, zuv, wo, wu, wd = _inproj(x, sc1, sh1, g_pre_mix[l], w_in[l].astype(BF16), d_ssm,
                                      (w_out[l], w_up[l], w_down[l]))
        tables = _ssm_tables(ssm_log_dt[l], ssm_a_re[l], ssm_a_im[l], ssm_b_re[l], ssm_b_im[l],
                             ssm_c_re[l], ssm_c_im[l], ssm_d[l], ssm_w_glu[l])
        y_ssm = _ssm(zs, tables, ssm_b_glu[l], g_out_ssm[l])
        w_s = jnp.tril(sgu_w[l]).astype(BF16)
        b_full = jnp.repeat(sgu_b[l].T, hd, axis=1)
        y_sgu = _sgu(zuv, sgu_ln_g[l], sgu_ln_b[l], w_s, b_full, g_out_sgu[l], heads)
        x, h = _outproj(y_ssm, y_sgu, x, gt1, g_post_mix[l], wo, sc2, sh2, g_pre_ffn[l])

        act = _ffn_up(h, wu, conv_w[l], conv_b[l])
        x = _ffn_down(act, x, gt2, g_post_ffn[l], wd)
    return x
```

```python
import functools
import math

import jax
import jax.numpy as jnp
import numpy as np
from jax import lax
from jax.experimental import pallas as pl
from jax.experimental.pallas import tpu as pltpu

F32 = jnp.float32
BF16 = jnp.bfloat16
EPS = 1e-6

LANES = 128
SSM_L = 16
SGU_CHUNK = 128
VMEM_LIMIT = 60 * 1024 * 1024


def _cparams(sem):
    return pltpu.CompilerParams(dimension_semantics=sem, vmem_limit_bytes=VMEM_LIMIT)


def _const_spec(shape):
    nd = len(shape)
    return pl.BlockSpec(shape, lambda *_: (0,) * nd, pipeline_mode=pl.Buffered(1))


def _rms(xf, g):
    return xf * lax.rsqrt(jnp.mean(xf * xf, axis=-1, keepdims=True) + EPS) * g


def _gelu_tanh(x):
    k = math.sqrt(2.0 / math.pi)
    hx = 0.5 * x
    return hx + hx * jnp.tanh(x * (k + (k * 0.044715) * (x * x)))


def _ada_kernel(c_ref, w_ref, b_ref, o_ref):
    c = c_ref[...]
    ca = (c * jax.nn.sigmoid(c)).astype(BF16)
    o_ref[...] = jnp.dot(ca, w_ref[...].astype(BF16), preferred_element_type=F32) + b_ref[...]


def _ada(c, w, b, tn=1024):
    bsz, d = c.shape
    n = w.shape[1]
    return pl.pallas_call(
        _ada_kernel,
        grid=(n // tn,),
        in_specs=[pl.BlockSpec((bsz, d), lambda j: (0, 0)),
                  pl.BlockSpec((d, tn), lambda j: (0, j)),
                  pl.BlockSpec((1, tn), lambda j: (0, j))],
        out_specs=pl.BlockSpec((bsz, tn), lambda j: (0, j)),
        out_shape=jax.ShapeDtypeStruct((bsz, n), F32),
        compiler_params=_cparams(("arbitrary",)),
        name="ada",
    )(c, w, b.reshape(1, n))


INPROJ_PIECES = 2


def _inproj_kernel(x_ref, sc_ref, sh_ref, g_ref, w_ref, *rest, d_ssm, n_cast):
    cast_in = rest[:n_cast]
    zs_ref, zuv_ref = rest[n_cast:n_cast + 2]
    cast_out = rest[n_cast + 2:]
    gs = g_ref[...] * (1.0 + sc_ref[0])
    tm = x_ref.shape[1]
    piece = tm // INPROJ_PIECES
    for r0 in range(0, tm, piece):
        rows = slice(r0, r0 + piece)
        h = _rms(x_ref[0, rows], gs) + sh_ref[0]
        z = jnp.dot(h.astype(BF16), w_ref[...], preferred_element_type=F32)
        for j in range(d_ssm // LANES):
            zs_ref[0, j, rows] = z[:, j * LANES:(j + 1) * LANES]
        zuv_ref[0, rows] = z[:, d_ssm:].astype(zuv_ref.dtype)
        if r0 == 0:
            for src, dst in zip(cast_in, cast_out):
                dst[...] = src[...].astype(dst.dtype)


def _inproj(x, sc, sh, g, w_bf16, d_ssm, later_weights, tm=512):
    bsz, s, d = x.shape
    d_in = w_bf16.shape[1]
    nslab = d_ssm // LANES
    nt = s // tm
    steps = bsz * nt
    wspecs = [pl.BlockSpec((w.shape[0] // steps, w.shape[1]), lambda b, i: (b * nt + i, 0))
              for w in later_weights]
    return pl.pallas_call(
        functools.partial(_inproj_kernel, d_ssm=d_ssm, n_cast=len(later_weights)),
        grid=(bsz, nt),
        in_specs=[pl.BlockSpec((1, tm, d), lambda b, i: (b, i, 0)),
                  pl.BlockSpec((1, 1, d), lambda b, i: (b, 0, 0)),
                  pl.BlockSpec((1, 1, d), lambda b, i: (b, 0, 0)),
                  _const_spec((1, d)),
                  _const_spec((d, d_in))] + wspecs,
        out_specs=[pl.BlockSpec((1, nslab, tm, LANES), lambda b, i: (b, 0, i, 0)),
                   pl.BlockSpec((1, tm, d_in - d_ssm), lambda b, i: (b, i, 0))] + wspecs,
        out_shape=[jax.ShapeDtypeStruct((bsz, nslab, s, LANES), F32),
                   jax.ShapeDtypeStruct((bsz, s, d_in - d_ssm), BF16)]
                  + [jax.ShapeDtypeStruct(w.shape, BF16) for w in later_weights],
        compiler_params=_cparams(("arbitrary", "arbitrary")),
        name="inproj",
    )(x, sc, sh, g.reshape(1, d), w_bf16, *later_weights)


def _toeplitz_kernel(k_ref, m_ref):
    gb, c, w = k_ref.shape
    lane = lax.broadcasted_iota(jnp.int32, (c, w), 1)
    for g in range(gb):
        kc = k_ref[g]
        for s in range(w // c):
            blk = kc if s == 0 else jnp.where(lane >= s * c, pltpu.roll(kc, s * c, 1), 0.0)
            m_ref[g, s * c:(s + 1) * c, :] = blk.astype(m_ref.dtype)


def _toeplitz(kcat, gb=8):
    g, c, w = kcat.shape
    return pl.pallas_call(
        _toeplitz_kernel,
        grid=(g // gb,),
        in_specs=[pl.BlockSpec((gb, c, w), lambda i: (i, 0, 0))],
        out_specs=pl.BlockSpec((gb, w, w), lambda i: (i, 0, 0)),
        out_shape=jax.ShapeDtypeStruct((g, w, w), BF16),
        compiler_params=_cparams(("arbitrary",)),
        name="toeplitz",
    )(kcat)


def _ssm_tables(log_dt, a_re, a_im, b_re, b_im, c_re, c_im, d, w_glu):
    hp = lax.Precision.HIGHEST
    L = SSM_L
    G, N, C = b_re.shape
    dt = jnp.exp(log_dt)[:, None]
    ar_dt = a_re * dt
    ai_dt = a_im * dt
    mag = jnp.exp(ar_dt)
    lb_re = mag * jnp.cos(ai_dt)
    lb_im = mag * jnp.sin(ai_dt)
    den = a_re * a_re + a_im * a_im
    nr = lb_re - 1.0
    ni = lb_im
    f_re = ((nr * a_re + ni * a_im) / den)[:, :, None]
    f_im = ((ni * a_re - nr * a_im) / den)[:, :, None]
    bb_re = f_re * b_re - f_im * b_im
    bb_im = f_re * b_im + f_im * b_re
    tau = jnp.arange(L + 1, dtype=F32)[None, :, None]
    pmag = jnp.exp(ar_dt[:, None, :] * tau)
    lp_re = pmag * jnp.cos(ai_dt[:, None, :] * tau)
    lp_im = pmag * jnp.sin(ai_dt[:, None, :] * tau)
    lq_re = jnp.repeat(lp_re[:, L - 1::-1], C, axis=1)
    lq_im = jnp.repeat(lp_im[:, L - 1::-1], C, axis=1)
    bt_re = jnp.tile(bb_re.transpose(0, 2, 1), (1, L, 1))
    bt_im = jnp.tile(bb_im.transpose(0, 2, 1), (1, L, 1))
    p = jnp.concatenate([lq_re * bt_re - lq_im * bt_im, lq_re * bt_im + lq_im * bt_re], axis=-1)
    ct_re = jnp.tile(c_re.transpose(0, 2, 1), (1, 1, L + 1))
    ct_im = jnp.tile(c_im.transpose(0, 2, 1), (1, 1, L + 1))
    lt_re = jnp.repeat(lp_re.transpose(0, 2, 1), C, axis=2)
    lt_im = jnp.repeat(lp_im.transpose(0, 2, 1), C, axis=2)
    v_re = ct_re * lt_re - ct_im * lt_im
    v_im = ct_re * lt_im + ct_im * lt_re
    q = jnp.concatenate([v_re[:, :, C:], -v_im[:, :, C:]], axis=1)
    kcat = (jnp.einsum("gni,gnx->gix", bb_re, v_re[:, :, :L * C], precision=hp)
            - jnp.einsum("gni,gnx->gix", bb_im, v_im[:, :, :L * C], precision=hp))
    skip = d.reshape(G, C)[:, :, None] * jnp.eye(C, dtype=F32)
    kcat = kcat + jnp.pad(skip, ((0, 0), (0, 0), (0, (L - 1) * C)))
    m = _toeplitz(kcat)
    l_re = lp_re[:, L]
    l_im = lp_im[:, L]
    a1 = jnp.concatenate([l_re, l_re], axis=-1)
    a2u = jnp.concatenate([-l_im, l_im], axis=-1)
    a2v = jnp.concatenate([l_im, -l_im], axis=-1)
    gps = 2 * LANES // C
    wg = w_glu.reshape(G // gps, gps, C, C)
    wg = (wg[:, :, :, None, :] * jnp.eye(gps, dtype=F32)[None, :, None, :, None]).reshape(
        G // gps, gps * C, gps * C)
    return m, p.astype(BF16), q.astype(BF16), a1, a2u, a2v, wg.astype(BF16)


def _block_swap_perm():
    idx = np.arange(8 * LANES)
    a, b, c = idx // LANES, (idx // SSM_L) % 8, idx % SSM_L
    return jnp.asarray(idx[None, :] == (b * LANES + a * SSM_L + c)[:, None], dtype=BF16)


def _ssm_kernel(zs_ref, perm_ref, m_ref, p_ref, q_ref, a1_ref, a2u_ref, a2v_ref, wg_ref, bg_ref, go_ref,
                o_ref, zf_ref, xu_ref, xv_ref, hp_ref, ys_ref, hu_ref, hv_ref, *, n, pitch, groups):
    L = SSM_L
    d = groups * L
    nslab = d // LANES

    @pl.when(pl.program_id(1) == 0)
    def _():
        hu_ref[...] = jnp.zeros_like(hu_ref)
        hv_ref[...] = jnp.zeros_like(hv_ref)

    def swap_blocks(pieces):
        lhs = jnp.concatenate([jnp.concatenate(ps, axis=1) for ps in pieces], axis=0)
        return jnp.dot(lhs, perm_ref[...], preferred_element_type=F32).astype(BF16)

    for hh in range(2):
        out = swap_blocks([[zs_ref[0, j, pl.ds(8 * hh + tp, n, stride=L), :].astype(BF16)
                            for tp in range(8)] for j in range(nslab)])
        for j in range(nslab):
            for qq in range(8):
                zf_ref[8 * j + qq, :, hh * LANES:(hh + 1) * LANES] = (
                    out[j * n:(j + 1) * n, qq * LANES:(qq + 1) * LANES])

    gx = 8

    def x_body(i, carry):
        for gi in range(gx):
            g = i * gx + gi
            xo = jnp.dot(zf_ref[g], p_ref[g], preferred_element_type=F32)
            r = pl.ds(pl.multiple_of(g * pitch, 8), n)
            xu_ref[r, :] = xo
            xv_ref[r, :] = pltpu.roll(xo, LANES // 2, 1)
        return carry

    lax.fori_loop(0, groups // gx, x_body, 0)

    a1 = a1_ref[...]
    a2u = a2u_ref[...]
    a2v = a2v_ref[...]
    ks = 8

    def scan_body(i, carry):
        hu, hv = carry
        for s in range(ks):
            kk = i * ks + s
            u = xu_ref[pl.ds(kk, groups, stride=pitch), :]
            v = xv_ref[pl.ds(kk, groups, stride=pitch), :]
            hp_ref[pl.ds(kk, groups, stride=pitch), :] = hu
            hu, hv = a1 * hu + a2u * hv + u, a1 * hv + a2v * hu + v
        return hu, hv

    hu, hv = lax.fori_loop(0, n // ks, scan_body, (hu_ref[...], hv_ref[...]))
    hu_ref[...] = hu
    hv_ref[...] = hv

    gy = 8

    def y_body(i, carry):
        for gi in range(gy):
            g = i * gy + gi
            hprev = hp_ref[pl.ds(pl.multiple_of(g * pitch, 8), n), :].astype(BF16)
            y = (jnp.dot(zf_ref[g], m_ref[g], preferred_element_type=F32)
                 + jnp.dot(hprev, q_ref[g], preferred_element_type=F32))
            zf_ref[g] = _gelu_tanh(y).astype(BF16)
        return carry

    lax.fori_loop(0, groups // gy, y_body, 0)

    for hh in range(2):
        out = swap_blocks([[zf_ref[8 * j + qq, :, hh * LANES:(hh + 1) * LANES] for qq in range(8)]
                           for j in range(nslab)])
        for j in range(nslab):
            for tp in range(8):
                ys_ref[:, (8 * hh + tp) * d + j * LANES:(8 * hh + tp) * d + (j + 1) * LANES] = (
                    out[j * n:(j + 1) * n, tp * LANES:(tp + 1) * LANES])

    w2 = 2 * LANES
    for t in range(L):
        yb = ys_ref[:, t * d:(t + 1) * d]
        gate = jnp.concatenate(
            [jnp.dot(yb[:, i * w2:(i + 1) * w2], wg_ref[i], preferred_element_type=F32)
             for i in range(d // w2)], axis=1) + bg_ref[...]
        y = _rms(yb.astype(F32) * jax.nn.sigmoid(gate), go_ref[...])
        for j in range(nslab):
            o_ref[0, j, pl.ds(t, n, stride=L), :] = y[:, j * LANES:(j + 1) * LANES]


def _ssm(zs, tables, b_glu, g_out, sb=1024):
    m, p, q, a1, a2u, a2v, wg = tables
    bsz, nslab, s, _ = zs.shape
    d_ssm = nslab * LANES
    groups = m.shape[0]
    L = SSM_L
    n = sb // L
    pitch = n + 8
    perm = _block_swap_perm()
    kern = functools.partial(_ssm_kernel, n=n, pitch=pitch, groups=groups)
    slabs = pl.BlockSpec((1, nslab, sb, LANES), lambda b, i: (b, 0, i, 0))
    return pl.pallas_call(
        kern,
        grid=(bsz, s // sb),
        in_specs=[slabs,
                  _const_spec(perm.shape),
                  _const_spec(m.shape), _const_spec(p.shape), _const_spec(q.shape),
                  _const_spec(a1.shape), _const_spec(a2u.shape), _const_spec(a2v.shape),
                  _const_spec(wg.shape), _const_spec((1, d_ssm)), _const_spec((1, d_ssm))],
        out_specs=slabs,
        out_shape=jax.ShapeDtypeStruct((bsz, nslab, s, LANES), F32),
        scratch_shapes=[pltpu.VMEM((groups, n, 2 * LANES), BF16),
                        pltpu.VMEM((groups * pitch, LANES), F32),
                        pltpu.VMEM((groups * pitch, LANES), F32),
                        pltpu.VMEM((groups * pitch, LANES), F32),
                        pltpu.VMEM((n, L * d_ssm), BF16),
                        pltpu.VMEM((groups, LANES), F32),
                        pltpu.VMEM((groups, LANES), F32)],
        compiler_params=_cparams(("arbitrary", "arbitrary")),
        name="ssm",
    )(zs, perm, m, p, q, a1, a2u, a2v, wg, b_glu.reshape(1, d_ssm), g_out.reshape(1, d_ssm))


def _sgu_kernel(zuv_ref, lng_ref, lnb_ref, w_ref, bs_ref, go_ref, o_ref, *, heads, hd):
    d = heads * hd
    ts = zuv_ref.shape[1]
    nc = ts // SGU_CHUNK
    u = _gelu_tanh(zuv_ref[0, :, :d].astype(F32))
    v = _gelu_tanh(zuv_ref[0, :, d:].astype(F32))
    mu = jnp.mean(v, axis=-1, keepdims=True)
    vc = v - mu
    v = vc * lax.rsqrt(jnp.mean(vc * vc, axis=-1, keepdims=True) + EPS) * lng_ref[...] + lnb_ref[...]
    vb = v.astype(BF16)
    cols = []
    for h in range(heads):
        vh = jnp.concatenate([vb[c * SGU_CHUNK:(c + 1) * SGU_CHUNK, h * hd:(h + 1) * hd]
                              for c in range(nc)], axis=1)
        mixed = jnp.dot(w_ref[h], vh, preferred_element_type=F32)
        mixed = jnp.concatenate([mixed[:, c * hd:(c + 1) * hd] for c in range(nc)], axis=0)
        cols.append(mixed)
    mixed = jnp.concatenate(cols, axis=1)
    bias = jnp.concatenate([bs_ref[...]] * nc, axis=0)
    y = u * (mixed + bias)
    o_ref[0] = _rms(y, go_ref[...]).astype(o_ref.dtype)


def _sgu(zuv, ln_g, ln_b, w_masked_bf16, b_full, g_out, heads, ts=512):
    bsz, s, d2 = zuv.shape
    d = d2 // 2
    hd = d // heads
    return pl.pallas_call(
        functools.partial(_sgu_kernel, heads=heads, hd=hd),
        grid=(bsz, s // ts),
        in_specs=[pl.BlockSpec((1, ts, d2), lambda b, i: (b, i, 0)),
                  _const_spec((1, d)), _const_spec((1, d)),
                  _const_spec(w_masked_bf16.shape), _const_spec(b_full.shape), _const_spec((1, d))],
        out_specs=pl.BlockSpec((1, ts, d), lambda b, i: (b, i, 0)),
        out_shape=jax.ShapeDtypeStruct((bsz, s, d), BF16),
        compiler_params=_cparams(("arbitrary", "arbitrary")),
        name="sgu",
    )(zuv, ln_g.reshape(1, d), ln_b.reshape(1, d), w_masked_bf16, b_full, g_out.reshape(1, d))


OUTPROJ_PIECES = 2


def _outproj_kernel(ya_ref, yb_ref, x_ref, gt_ref, g_ref, wa_ref, wb_ref, sc_ref, sh_ref, gn_ref,
                    o_ref, h_ref):
    piece = x_ref.shape[1] // OUTPROJ_PIECES
    gg = g_ref[...] * gt_ref[0]
    gs = gn_ref[...] * (1.0 + sc_ref[0])
    for r0 in range(0, x_ref.shape[1], piece):
        rows = slice(r0, r0 + piece)
        ya = jnp.concatenate([ya_ref[0, j, rows] for j in range(ya_ref.shape[1])], axis=1).astype(BF16)
        y = (jnp.dot(ya, wa_ref[...], preferred_element_type=F32)
             + jnp.dot(yb_ref[0, rows], wb_ref[...], preferred_element_type=F32))
        x1 = x_ref[0, rows] + _rms(y, gg)
        o_ref[0, rows] = x1
        h_ref[0, rows] = (_rms(x1, gs) + sh_ref[0]).astype(h_ref.dtype)


def _outproj(ya, yb, x, gt, g, w_bf16, sc_next, sh_next, g_next, tm=512):
    bsz, s, d = x.shape
    nslab = ya.shape[1]
    db = yb.shape[-1]
    assert nslab * LANES == db and w_bf16.shape[0] == 2 * db
    row = lambda w: pl.BlockSpec((1, tm, w), lambda b, i: (b, i, 0))
    per_batch = pl.BlockSpec((1, 1, d), lambda b, i: (b, 0, 0))
    half = lambda k: pl.BlockSpec((db, d), lambda b, i: (k, 0), pipeline_mode=pl.Buffered(1))
    return pl.pallas_call(
        _outproj_kernel,
        grid=(bsz, s // tm),
        in_specs=[pl.BlockSpec((1, nslab, tm, LANES), lambda b, i: (b, 0, i, 0)), row(db), row(d),
                  per_batch, _const_spec((1, d)), half(0), half(1),
                  per_batch, per_batch, _const_spec((1, d))],
        out_specs=[row(d), row(d)],
        out_shape=[jax.ShapeDtypeStruct((bsz, s, d), F32), jax.ShapeDtypeStruct((bsz, s, d), BF16)],
        compiler_params=_cparams(("arbitrary", "arbitrary")),
        name="outproj",
    )(ya, yb, x, gt, g.reshape(1, d), w_bf16, w_bf16, sc_next, sh_next, g_next.reshape(1, d))


FFN_HALO = 16
def _ffn_up_kernel(hin_ref, hh_ref, wa_ref, wb_ref, cwa_ref, cwb_ref, cba_ref, cbb_ref, o_ref, h_ref):
    i = pl.program_id(1)
    j = pl.program_id(2)

    @pl.when(j == 0)
    def _():
        halo = hh_ref[0]
        h_ref[:FFN_HALO, :] = jnp.where(i > 0, halo, jnp.zeros_like(halo))
        h_ref[FFN_HALO:, :] = hin_ref[0]

    h = h_ref[...]

    def branch(w_ref, cw_ref, cb_ref):
        up = jnp.dot(h, w_ref[...], preferred_element_type=F32)
        cw = cw_ref[...]
        r = (cw[0:1] * pltpu.roll(up, 2, 0) + cw[1:2] * pltpu.roll(up, 1, 0) + cw[2:3] * up)
        return r[FFN_HALO:] + cb_ref[...]

    a = branch(wa_ref, cwa_ref, cba_ref)
    b = branch(wb_ref, cwb_ref, cbb_ref)
    o_ref[0] = (a * jax.nn.sigmoid(a) * b).astype(o_ref.dtype)


def _ffn_up(h, w_up_bf16, conv_w, conv_b, tm=1024, tf=512):
    bsz, s, d = h.shape
    f = w_up_bf16.shape[1] // 2
    nf = f // tf
    hb = tm // FFN_HALO
    cb = conv_b.reshape(1, 2 * f)
    return pl.pallas_call(
        _ffn_up_kernel,
        grid=(bsz, s // tm, nf),
        in_specs=[pl.BlockSpec((1, tm, d), lambda b, i, j: (b, i, 0)),
                  pl.BlockSpec((1, FFN_HALO, d), lambda b, i, j: (b, jnp.maximum(i * hb - 1, 0), 0)),
                  pl.BlockSpec((d, tf), lambda b, i, j: (0, j)),
                  pl.BlockSpec((d, tf), lambda b, i, j: (0, nf + j)),
                  pl.BlockSpec((3, tf), lambda b, i, j: (0, j)),
                  pl.BlockSpec((3, tf), lambda b, i, j: (0, nf + j)),
                  pl.BlockSpec((1, tf), lambda b, i, j: (0, j)),
                  pl.BlockSpec((1, tf), lambda b, i, j: (0, nf + j))],
        out_specs=pl.BlockSpec((1, tm, tf), lambda b, i, j: (b, i, j)),
        out_shape=jax.ShapeDtypeStruct((bsz, s, f), BF16),
        scratch_shapes=[pltpu.VMEM((FFN_HALO + tm, d), BF16)],
        compiler_params=_cparams(("arbitrary", "arbitrary", "arbitrary")),
        name="ffn_up",
    )(h, h, w_up_bf16, w_up_bf16, conv_w, conv_w, cb, cb)


def _ffn_down_kernel(act_ref, x_ref, gt_ref, g_ref, wd_ref, o_ref):
    f = jnp.dot(act_ref[0], wd_ref[...], preferred_element_type=F32)
    o_ref[0] = x_ref[0] + _rms(f, g_ref[...] * gt_ref[0])


def _ffn_down(act, x, gt, g_post, w_down_bf16, tm=256):
    bsz, s, d = x.shape
    f = act.shape[-1]
    return pl.pallas_call(
        _ffn_down_kernel,
        grid=(bsz, s // tm),
        in_specs=[pl.BlockSpec((1, tm, f), lambda b, i: (b, i, 0)),
                  pl.BlockSpec((1, tm, d), lambda b, i: (b, i, 0)),
                  pl.BlockSpec((1, 1, d), lambda b, i: (b, 0, 0)),
                  _const_spec((1, d)), _const_spec((f, d))],
        out_specs=pl.BlockSpec((1, tm, d), lambda b, i: (b, i, 0)),
        out_shape=jax.ShapeDtypeStruct((bsz, s, d), F32),
        compiler_params=_cparams(("arbitrary", "arbitrary")),
        name="ffn_down",
    )(act, x, gt, g_post.reshape(1, d), w_down_bf16)


def kernel(x, c, w_ada, b_ada, g_pre_mix, g_post_mix, w_in, ssm_log_dt, ssm_a_re, ssm_a_im, ssm_b_re, ssm_b_im, ssm_c_re, ssm_c_im, ssm_d, ssm_w_glu, ssm_b_glu, sgu_ln_g, sgu_ln_b, sgu_w, sgu_b, g_out_ssm, g_out_sgu, w_out, g_pre_ffn, g_post_ffn, w_up, conv_w, conv_b, w_down):
    depth = w_ada.shape[0]
    d = x.shape[-1]
    d_ssm = ssm_d.shape[-1]
    heads = sgu_w.shape[1]
    hd = (w_out.shape[1] - d_ssm) // heads
    for l in range(depth):
        mod = _ada(c, w_ada[l], b_ada[l])
        sh1, sc1, gt1, sh2, sc2, gt2 = [mod[:, None, k * d:(k + 1) * d] for k in range(6)]

        zs, zuv, wo, wu, wd = _inproj(x, sc1, sh1, g_pre_mix[l], w_in[l].astype(BF16), d_ssm,
                                      (w_out[l], w_up[l], w_down[l]))
        tables = _ssm_tables(ssm_log_dt[l], ssm_a_re[l], ssm_a_im[l], ssm_b_re[l], ssm_b_im[l],
                             ssm_c_re[l], ssm_c_im[l], ssm_d[l], ssm_w_glu[l])
        y_ssm = _ssm(zs, tables, ssm_b_glu[l], g_out_ssm[l])
        w_s = jnp.tril(sgu_w[l]).astype(BF16)
        b_full = jnp.repeat(sgu_b[l].T, hd, axis=1)
        y_sgu = _sgu(zuv, sgu_ln_g[l], sgu_ln_b[l], w_s, b_full, g_out_sgu[l], heads)
        x, h = _outproj(y_ssm, y_sgu, x, gt1, g_post_mix[l], wo, sc2, sh2, g_pre_ffn[l])

        act = _ffn_up(h, wu, conv_w[l], conv_b[l])
        x = _ffn_down(act, x, gt2, g_post_ffn[l], wd)
    return x
```

```python
import functools
import math

import jax
import jax.numpy as jnp
import numpy as np
from jax import lax
from jax.experimental import pallas as pl
from jax.experimental.pallas import tpu as pltpu

F32 = jnp.float32
BF16 = jnp.bfloat16
EPS = 1e-6

LANES = 128
SSM_L = 16
SGU_CHUNK = 128
VMEM_LIMIT = 60 * 1024 * 1024


def _cparams(sem):
    return pltpu.CompilerParams(dimension_semantics=sem, vmem_limit_bytes=VMEM_LIMIT)


def _const_spec(shape):
    nd = len(shape)
    return pl.BlockSpec(shape, lambda *_: (0,) * nd, pipeline_mode=pl.Buffered(1))


def _rms(xf, g):
    return xf * lax.rsqrt(jnp.mean(xf * xf, axis=-1, keepdims=True) + EPS) * g


def _gelu_tanh(x):
    k = math.sqrt(2.0 / math.pi)
    hx = 0.5 * x
    return hx + hx * jnp.tanh(x * (k + (k * 0.044715) * (x * x)))


def _ada_kernel(c_ref, w_ref, b_ref, o_ref):
    c = c_ref[...]
    ca = (c * jax.nn.sigmoid(c)).astype(BF16)
    o_ref[...] = jnp.dot(ca, w_ref[...].astype(BF16), preferred_element_type=F32) + b_ref[...]


def _ada(c, w, b, tn=2048):
    bsz, d = c.shape
    n = w.shape[1]
    return pl.pallas_call(
        _ada_kernel,
        grid=(n // tn,),
        in_specs=[pl.BlockSpec((bsz, d), lambda j: (0, 0)),
                  pl.BlockSpec((d, tn), lambda j: (0, j)),
                  pl.BlockSpec((1, tn), lambda j: (0, j))],
        out_specs=pl.BlockSpec((bsz, tn), lambda j: (0, j)),
        out_shape=jax.ShapeDtypeStruct((bsz, n), F32),
        compiler_params=_cparams(("arbitrary",)),
        name="ada",
    )(c, w, b.reshape(1, n))


INPROJ_PIECES = 2


def _inproj_kernel(x_ref, sc_ref, sh_ref, g_ref, w_ref, *rest, d_ssm, n_cast):
    cast_in = rest[:n_cast]
    zs_ref, zuv_ref = rest[n_cast:n_cast + 2]
    cast_out = rest[n_cast + 2:]
    gs = g_ref[...] * (1.0 + sc_ref[0])
    tm = x_ref.shape[1]
    piece = tm // INPROJ_PIECES
    for r0 in range(0, tm, piece):
        rows = slice(r0, r0 + piece)
        h = _rms(x_ref[0, rows], gs) + sh_ref[0]
        z = jnp.dot(h.astype(BF16), w_ref[...], preferred_element_type=F32)
        for j in range(d_ssm // LANES):
            zs_ref[0, j, rows] = z[:, j * LANES:(j + 1) * LANES]
        zuv_ref[0, rows] = z[:, d_ssm:].astype(zuv_ref.dtype)
        if r0 == 0:
            for src, dst in zip(cast_in, cast_out):
                dst[...] = src[...].astype(dst.dtype)


def _inproj(x, sc, sh, g, w_bf16, d_ssm, later_weights, tm=512):
    bsz, s, d = x.shape
    d_in = w_bf16.shape[1]
    nslab = d_ssm // LANES
    nt = s // tm
    steps = bsz * nt
    wspecs = [pl.BlockSpec((w.shape[0] // steps, w.shape[1]), lambda b, i: (b * nt + i, 0))
              for w in later_weights]
    return pl.pallas_call(
        functools.partial(_inproj_kernel, d_ssm=d_ssm, n_cast=len(later_weights)),
        grid=(bsz, nt),
        in_specs=[pl.BlockSpec((1, tm, d), lambda b, i: (b, i, 0)),
                  pl.BlockSpec((1, 1, d), lambda b, i: (b, 0, 0)),
                  pl.BlockSpec((1, 1, d), lambda b, i: (b, 0, 0)),
                  _const_spec((1, d)),
                  _const_spec((d, d_in))] + wspecs,
        out_specs=[pl.BlockSpec((1, nslab, tm, LANES), lambda b, i: (b, 0, i, 0)),
                   pl.BlockSpec((1, tm, d_in - d_ssm), lambda b, i: (b, i, 0))] + wspecs,
        out_shape=[jax.ShapeDtypeStruct((bsz, nslab, s, LANES), F32),
                   jax.ShapeDtypeStruct((bsz, s, d_in - d_ssm), BF16)]
                  + [jax.ShapeDtypeStruct(w.shape, BF16) for w in later_weights],
        compiler_params=_cparams(("arbitrary", "arbitrary")),
        name="inproj",
    )(x, sc, sh, g.reshape(1, d), w_bf16, *later_weights)


def _toeplitz_kernel(k_ref, m_ref):
    gb, c, w = k_ref.shape
    lane = lax.broadcasted_iota(jnp.int32, (c, w), 1)
    for g in range(gb):
        kc = k_ref[g]
        for s in range(w // c):
            blk = kc if s == 0 else jnp.where(lane >= s * c, pltpu.roll(kc, s * c, 1), 0.0)
            m_ref[g, s * c:(s + 1) * c, :] = blk.astype(m_ref.dtype)


def _toeplitz(kcat, gb=8):
    g, c, w = kcat.shape
    return pl.pallas_call(
        _toeplitz_kernel,
        grid=(g // gb,),
        in_specs=[pl.BlockSpec((gb, c, w), lambda i: (i, 0, 0))],
        out_specs=pl.BlockSpec((gb, w, w), lambda i: (i, 0, 0)),
        out_shape=jax.ShapeDtypeStruct((g, w, w), BF16),
        compiler_params=_cparams(("arbitrary",)),
        name="toeplitz",
    )(kcat)


def _ssm_tables(log_dt, a_re, a_im, b_re, b_im, c_re, c_im, d, w_glu):
    hp = lax.Precision.HIGHEST
    L = SSM_L
    G, N, C = b_re.shape
    dt = jnp.exp(log_dt)[:, None]
    ar_dt = a_re * dt
    ai_dt = a_im * dt
    mag = jnp.exp(ar_dt)
    lb_re = mag * jnp.cos(ai_dt)
    lb_im = mag * jnp.sin(ai_dt)
    den = a_re * a_re + a_im * a_im
    nr = lb_re - 1.0
    ni = lb_im
    f_re = ((nr * a_re + ni * a_im) / den)[:, :, None]
    f_im = ((ni * a_re - nr * a_im) / den)[:, :, None]
    bb_re = f_re * b_re - f_im * b_im
    bb_im = f_re * b_im + f_im * b_re
    tau = jnp.arange(L + 1, dtype=F32)[None, :, None]
    pmag = jnp.exp(ar_dt[:, None, :] * tau)
    lp_re = pmag * jnp.cos(ai_dt[:, None, :] * tau)
    lp_im = pmag * jnp.sin(ai_dt[:, None, :] * tau)
    lq_re = jnp.repeat(lp_re[:, L - 1::-1], C, axis=1)
    lq_im = jnp.repeat(lp_im[:, L - 1::-1], C, axis=1)
    bt_re = jnp.tile(bb_re.transpose(0, 2, 1), (1, L, 1))
    bt_im = jnp.tile(bb_im.transpose(0, 2, 1), (1, L, 1))
    p = jnp.concatenate([lq_re * bt_re - lq_im * bt_im, lq_re * bt_im + lq_im * bt_re], axis=-1)
    ct_re = jnp.tile(c_re.transpose(0, 2, 1), (1, 1, L + 1))
    ct_im = jnp.tile(c_im.transpose(0, 2, 1), (1, 1, L + 1))
    lt_re = jnp.repeat(lp_re.transpose(0, 2, 1), C, axis=2)
    lt_im = jnp.repeat(lp_im.transpose(0, 2, 1), C, axis=2)
    v_re = ct_re * lt_re - ct_im * lt_im
    v_im = ct_re * lt_im + ct_im * lt_re
    q = jnp.concatenate([v_re[:, :, C:], -v_im[:, :, C:]], axis=1)
    kcat = (jnp.einsum("gni,gnx->gix", bb_re, v_re[:, :, :L * C], precision=hp)
            - jnp.einsum("gni,gnx->gix", bb_im, v_im[:, :, :L * C], precision=hp))
    skip = d.reshape(G, C)[:, :, None] * jnp.eye(C, dtype=F32)
    kcat = kcat + jnp.pad(skip, ((0, 0), (0, 0), (0, (L - 1) * C)))
    m = _toeplitz(kcat)
    l_re = lp_re[:, L]
    l_im = lp_im[:, L]
    a1 = jnp.concatenate([l_re, l_re], axis=-1)
    a2u = jnp.concatenate([-l_im, l_im], axis=-1)
    a2v = jnp.concatenate([l_im, -l_im], axis=-1)
    gps = 2 * LANES // C
    wg = w_glu.reshape(G // gps, gps, C, C)
    wg = (wg[:, :, :, None, :] * jnp.eye(gps, dtype=F32)[None, :, None, :, None]).reshape(
        G // gps, gps * C, gps * C)
    return m, p.astype(BF16), q.astype(BF16), a1, a2u, a2v, wg.astype(BF16)


def _block_swap_perm():
    idx = np.arange(8 * LANES)
    a, b, c = idx // LANES, (idx // SSM_L) % 8, idx % SSM_L
    return jnp.asarray(idx[None, :] == (b * LANES + a * SSM_L + c)[:, None], dtype=BF16)


def _ssm_kernel(zs_ref, perm_ref, m_ref, p_ref, q_ref, a1_ref, a2u_ref, a2v_ref, wg_ref, bg_ref, go_ref,
                o_ref, zf_ref, xu_ref, xv_ref, hp_ref, ys_ref, hu_ref, hv_ref, *, n, pitch, groups):
    L = SSM_L
    d = groups * L
    nslab = d // LANES

    @pl.when(pl.program_id(1) == 0)
    def _():
        hu_ref[...] = jnp.zeros_like(hu_ref)
        hv_ref[...] = jnp.zeros_like(hv_ref)

    def swap_blocks(pieces):
        lhs = jnp.concatenate([jnp.concatenate(ps, axis=1) for ps in pieces], axis=0)
        return jnp.dot(lhs, perm_ref[...], preferred_element_type=F32).astype(BF16)

    for hh in range(2):
        out = swap_blocks([[zs_ref[0, j, pl.ds(8 * hh + tp, n, stride=L), :].astype(BF16)
                            for tp in range(8)] for j in range(nslab)])
        for j in range(nslab):
            for qq in range(8):
                zf_ref[8 * j + qq, :, hh * LANES:(hh + 1) * LANES] = (
                    out[j * n:(j + 1) * n, qq * LANES:(qq + 1) * LANES])

    gx = 8

    def x_body(i, carry):
        for gi in range(gx):
            g = i * gx + gi
            xo = jnp.dot(zf_ref[g], p_ref[g], preferred_element_type=F32)
            r = pl.ds(pl.multiple_of(g * pitch, 8), n)
            xu_ref[r, :] = xo
            xv_ref[r, :] = pltpu.roll(xo, LANES // 2, 1)
        return carry

    lax.fori_loop(0, groups // gx, x_body, 0)

    a1 = a1_ref[...]
    a2u = a2u_ref[...]
    a2v = a2v_ref[...]
    ks = 8

    def scan_body(i, carry):
        hu, hv = carry
        for s in range(ks):
            kk = i * ks + s
            u = xu_ref[pl.ds(kk, groups, stride=pitch), :]
            v = xv_ref[pl.ds(kk, groups, stride=pitch), :]
            hp_ref[pl.ds(kk, groups, stride=pitch), :] = hu
            hu, hv = a1 * hu + a2u * hv + u, a1 * hv + a2v * hu + v
        return hu, hv

    hu, hv = lax.fori_loop(0, n // ks, scan_body, (hu_ref[...], hv_ref[...]))
    hu_ref[...] = hu
    hv_ref[...] = hv

    gy = 8

    def y_body(i, carry):
        for gi in range(gy):
            g = i * gy + gi
            hprev = hp_ref[pl.ds(pl.multiple_of(g * pitch, 8), n), :].astype(BF16)
            y = (jnp.dot(zf_ref[g], m_ref[g], preferred_element_type=F32)
                 + jnp.dot(hprev, q_ref[g], preferred_element_type=F32))
            zf_ref[g] = _gelu_tanh(y).astype(BF16)
        return carry

    lax.fori_loop(0, groups // gy, y_body, 0)

    for hh in range(2):
        out = swap_blocks([[zf_ref[8 * j + qq, :, hh * LANES:(hh + 1) * LANES] for qq in range(8)]
                           for j in range(nslab)])
        for j in range(nslab):
            for tp in range(8):
                ys_ref[:, (8 * hh + tp) * d + j * LANES:(8 * hh + tp) * d + (j + 1) * LANES] = (
                    out[j * n:(j + 1) * n, tp * LANES:(tp + 1) * LANES])

    w2 = 2 * LANES
    for t in range(L):
        yb = ys_ref[:, t * d:(t + 1) * d]
        gate = jnp.concatenate(
            [jnp.dot(yb[:, i * w2:(i + 1) * w2], wg_ref[i], preferred_element_type=F32)
             for i in range(d // w2)], axis=1) + bg_ref[...]
        y = _rms(yb.astype(F32) * jax.nn.sigmoid(gate), go_ref[...])
        for j in range(nslab):
            o_ref[0, j, pl.ds(t, n, stride=L), :] = y[:, j * LANES:(j + 1) * LANES]


def _ssm(zs, tables, b_glu, g_out, sb=1024):
    m, p, q, a1, a2u, a2v, wg = tables
    bsz, nslab, s, _ = zs.shape
    d_ssm = nslab * LANES
    groups = m.shape[0]
    L = SSM_L
    n = sb // L
    pitch = n + 8
    perm = _block_swap_perm()
    kern = functools.partial(_ssm_kernel, n=n, pitch=pitch, groups=groups)
    slabs = pl.BlockSpec((1, nslab, sb, LANES), lambda b, i: (b, 0, i, 0))
    return pl.pallas_call(
        kern,
        grid=(bsz, s // sb),
        in_specs=[slabs,
                  _const_spec(perm.shape),
                  _const_spec(m.shape), _const_spec(p.shape), _const_spec(q.shape),
                  _const_spec(a1.shape), _const_spec(a2u.shape), _const_spec(a2v.shape),
                  _const_spec(wg.shape), _const_spec((1, d_ssm)), _const_spec((1, d_ssm))],
        out_specs=slabs,
        out_shape=jax.ShapeDtypeStruct((bsz, nslab, s, LANES), F32),
        scratch_shapes=[pltpu.VMEM((groups, n, 2 * LANES), BF16),
                        pltpu.VMEM((groups * pitch, LANES), F32),
                        pltpu.VMEM((groups * pitch, LANES), F32),
                        pltpu.VMEM((groups * pitch, LANES), F32),
                        pltpu.VMEM((n, L * d_ssm), BF16),
                        pltpu.VMEM((groups, LANES), F32),
                        pltpu.VMEM((groups, LANES), F32)],
        compiler_params=_cparams(("arbitrary", "arbitrary")),
        name="ssm",
    )(zs, perm, m, p, q, a1, a2u, a2v, wg, b_glu.reshape(1, d_ssm), g_out.reshape(1, d_ssm))


def _sgu_kernel(zuv_ref, lng_ref, lnb_ref, w_ref, bs_ref, go_ref, o_ref, *, heads, hd):
    d = heads * hd
    ts = zuv_ref.shape[1]
    nc = ts // SGU_CHUNK
    u = _gelu_tanh(zuv_ref[0, :, :d].astype(F32))
    v = _gelu_tanh(zuv_ref[0, :, d:].astype(F32))
    mu = jnp.mean(v, axis=-1, keepdims=True)
    vc = v - mu
    v = vc * lax.rsqrt(jnp.mean(vc * vc, axis=-1, keepdims=True) + EPS) * lng_ref[...] + lnb_ref[...]
    vb = v.astype(BF16)
    cols = []
    for h in range(heads):
        vh = jnp.concatenate([vb[c * SGU_CHUNK:(c + 1) * SGU_CHUNK, h * hd:(h + 1) * hd]
                              for c in range(nc)], axis=1)
        mixed = jnp.dot(w_ref[h], vh, preferred_element_type=F32)
        mixed = jnp.concatenate([mixed[:, c * hd:(c + 1) * hd] for c in range(nc)], axis=0)
        cols.append(mixed)
    mixed = jnp.concatenate(cols, axis=1)
    bias = jnp.concatenate([bs_ref[...]] * nc, axis=0)
    y = u * (mixed + bias)
    o_ref[0] = _rms(y, go_ref[...]).astype(o_ref.dtype)


def _sgu(zuv, ln_g, ln_b, w_masked_bf16, b_full, g_out, heads, ts=512):
    bsz, s, d2 = zuv.shape
    d = d2 // 2
    hd = d // heads
    return pl.pallas_call(
        functools.partial(_sgu_kernel, heads=heads, hd=hd),
        grid=(bsz, s // ts),
        in_specs=[pl.BlockSpec((1, ts, d2), lambda b, i: (b, i, 0)),
                  _const_spec((1, d)), _const_spec((1, d)),
                  _const_spec(w_masked_bf16.shape), _const_spec(b_full.shape), _const_spec((1, d))],
        out_specs=pl.BlockSpec((1, ts, d), lambda b, i: (b, i, 0)),
        out_shape=jax.ShapeDtypeStruct((bsz, s, d), BF16),
        compiler_params=_cparams(("arbitrary", "arbitrary")),
        name="sgu",
    )(zuv, ln_g.reshape(1, d), ln_b.reshape(1, d), w_masked_bf16, b_full, g_out.reshape(1, d))


OUTPROJ_PIECES = 4


def _outproj_kernel(ya_ref, yb_ref, x_ref, gt_ref, g_ref, wa_ref, wb_ref, sc_ref, sh_ref, gn_ref,
                    o_ref, h_ref):
    piece = x_ref.shape[1] // OUTPROJ_PIECES
    gg = g_ref[...] * gt_ref[0]
    gs = gn_ref[...] * (1.0 + sc_ref[0])
    for r0 in range(0, x_ref.shape[1], piece):
        rows = slice(r0, r0 + piece)
        ya = jnp.concatenate([ya_ref[0, j, rows] for j in range(ya_ref.shape[1])], axis=1).astype(BF16)
        y = (jnp.dot(ya, wa_ref[...], preferred_element_type=F32)
             + jnp.dot(yb_ref[0, rows], wb_ref[...], preferred_element_type=F32))
        x1 = x_ref[0, rows] + _rms(y, gg)
        o_ref[0, rows] = x1
        h_ref[0, rows] = (_rms(x1, gs) + sh_ref[0]).astype(h_ref.dtype)


def _outproj(ya, yb, x, gt, g, w_bf16, sc_next, sh_next, g_next, tm=512):
    bsz, s, d = x.shape
    nslab = ya.shape[1]
    db = yb.shape[-1]
    assert nslab * LANES == db and w_bf16.shape[0] == 2 * db
    row = lambda w: pl.BlockSpec((1, tm, w), lambda b, i: (b, i, 0))
    per_batch = pl.BlockSpec((1, 1, d), lambda b, i: (b, 0, 0))
    half = lambda k: pl.BlockSpec((db, d), lambda b, i: (k, 0), pipeline_mode=pl.Buffered(1))
    return pl.pallas_call(
        _outproj_kernel,
        grid=(bsz, s // tm),
        in_specs=[pl.BlockSpec((1, nslab, tm, LANES), lambda b, i: (b, 0, i, 0)), row(db), row(d),
                  per_batch, _const_spec((1, d)), half(0), half(1),
                  per_batch, per_batch, _const_spec((1, d))],
        out_specs=[row(d), row(d)],
        out_shape=[jax.ShapeDtypeStruct((bsz, s, d), F32), jax.ShapeDtypeStruct((bsz, s, d), BF16)],
        compiler_params=_cparams(("arbitrary", "arbitrary")),
        name="outproj",
    )(ya, yb, x, gt, g.reshape(1, d), w_bf16, w_bf16, sc_next, sh_next, g_next.reshape(1, d))


FFN_HALO = 16
def _ffn_up_kernel(hin_ref, hh_ref, wa_ref, wb_ref, cwa_ref, cwb_ref, cba_ref, cbb_ref, o_ref, h_ref):
    i = pl.program_id(1)
    j = pl.program_id(2)

    @pl.when(j == 0)
    def _():
        halo = hh_ref[0]
        h_ref[:FFN_HALO, :] = jnp.where(i > 0, halo, jnp.zeros_like(halo))
        h_ref[FFN_HALO:, :] = hin_ref[0]

    h = h_ref[...]

    def branch(w_ref, cw_ref, cb_ref):
        up = jnp.dot(h, w_ref[...], preferred_element_type=F32)
        cw = cw_ref[...]
        r = (cw[0:1] * pltpu.roll(up, 2, 0) + cw[1:2] * pltpu.roll(up, 1, 0) + cw[2:3] * up)
        return r[FFN_HALO:] + cb_ref[...]

    a = branch(wa_ref, cwa_ref, cba_ref)
    b = branch(wb_ref, cwb_ref, cbb_ref)
    o_ref[0] = (a * jax.nn.sigmoid(a) * b).astype(o_ref.dtype)


def _ffn_up(h, w_up_bf16, conv_w, conv_b, tm=1024, tf=512):
    bsz, s, d = h.shape
    f = w_up_bf16.shape[1] // 2
    nf = f // tf
    hb = tm // FFN_HALO
    cb = conv_b.reshape(1, 2 * f)
    return pl.pallas_call(
        _ffn_up_kernel,
        grid=(bsz, s // tm, nf),
        in_specs=[pl.BlockSpec((1, tm, d), lambda b, i, j: (b, i, 0)),
                  pl.BlockSpec((1, FFN_HALO, d), lambda b, i, j: (b, jnp.maximum(i * hb - 1, 0), 0)),
                  pl.BlockSpec((d, tf), lambda b, i, j: (0, j)),
                  pl.BlockSpec((d, tf), lambda b, i, j: (0, nf + j)),
                  pl.BlockSpec((3, tf), lambda b, i, j: (0, j)),
                  pl.BlockSpec((3, tf), lambda b, i, j: (0, nf + j)),
                  pl.BlockSpec((1, tf), lambda b, i, j: (0, j)),
                  pl.BlockSpec((1, tf), lambda b, i, j: (0, nf + j))],
        out_specs=pl.BlockSpec((1, tm, tf), lambda b, i, j: (b, i, j)),
        out_shape=jax.ShapeDtypeStruct((bsz, s, f), BF16),
        scratch_shapes=[pltpu.VMEM((FFN_HALO + tm, d), BF16)],
        compiler_params=_cparams(("arbitrary", "arbitrary", "arbitrary")),
        name="ffn_up",
    )(h, h, w_up_bf16, w_up_bf16, conv_w, conv_w, cb, cb)


FFN_DOWN_PIECES = 2


def _ffn_down_kernel(act_ref, x_ref, gt_ref, g_ref, wd_ref, o_ref):
    gg = g_ref[...] * gt_ref[0]
    piece = x_ref.shape[1] // FFN_DOWN_PIECES
    for r0 in range(0, x_ref.shape[1], piece):
        rows = slice(r0, r0 + piece)
        f = jnp.dot(act_ref[0, rows], wd_ref[...], preferred_element_type=F32)
        o_ref[0, rows] = x_ref[0, rows] + _rms(f, gg)


def _ffn_down(act, x, gt, g_post, w_down_bf16, tm=512):
    bsz, s, d = x.shape
    f = act.shape[-1]
    return pl.pallas_call(
        _ffn_down_kernel,
        grid=(bsz, s // tm),
        in_specs=[pl.BlockSpec((1, tm, f), lambda b, i: (b, i, 0)),
                  pl.BlockSpec((1, tm, d), lambda b, i: (b, i, 0)),
                  pl.BlockSpec((1, 1, d), lambda b, i: (b, 0, 0)),
                  _const_spec((1, d)), _const_spec((f, d))],
        out_specs=pl.BlockSpec((1, tm, d), lambda b, i: (b, i, 0)),
        out_shape=jax.ShapeDtypeStruct((bsz, s, d), F32),
        compiler_params=_cparams(("arbitrary", "arbitrary")),
        name="ffn_down",
    )(act, x, gt, g_post.reshape(1, d), w_down_bf16)


def kernel(x, c, w_ada, b_ada, g_pre_mix, g_post_mix, w_in, ssm_log_dt, ssm_a_re, ssm_a_im, ssm_b_re, ssm_b_im, ssm_c_re, ssm_c_im, ssm_d, ssm_w_glu, ssm_b_glu, sgu_ln_g, sgu_ln_b, sgu_w, sgu_b, g_out_ssm, g_out_sgu, w_out, g_pre_ffn, g_post_ffn, w_up, conv_w, conv_b, w_down):
    depth = w_ada.shape[0]
    d = x.shape[-1]
    d_ssm = ssm_d.shape[-1]
    heads = sgu_w.shape[1]
    hd = (w_out.shape[1] - d_ssm) // heads
    for l in range(depth):
        mod = _ada(c, w_ada[l], b_ada[l])
        sh1, sc1, gt1, sh2, sc2, gt2 = [mod[:, None, k * d:(k + 1) * d] for k in range(6)]

        zs, zuv, wo, wu, wd = _inproj(x, sc1, sh1, g_pre_mix[l], w_in[l].astype(BF16), d_ssm,
                                      (w_out[l], w_up[l], w_down[l]))
        tables = _ssm_tables(ssm_log_dt[l], ssm_a_re[l], ssm_a_im[l], ssm_b_re[l], ssm_b_im[l],
                             ssm_c_re[l], ssm_c_im[l], ssm_d[l], ssm_w_glu[l])
        y_ssm = _ssm(zs, tables, ssm_b_glu[l], g_out_ssm[l])
        w_s = jnp.tril(sgu_w[l]).astype(BF16)
        b_full = jnp.repeat(sgu_b[l].T, hd, axis=1)
        y_sgu = _sgu(zuv, sgu_ln_g[l], sgu_ln_b[l], w_s, b_full, g_out_sgu[l], heads)
        x, h = _outproj(y_ssm, y_sgu, x, gt1, g_post_mix[l], wo, sc2, sh2, g_pre_ffn[l])

        act = _ffn_up(h, wu, conv_w[l], conv_b[l])
        x = _ffn_down(act, x, gt2, g_post_ffn[l], wd)
    return x
```

```python
import functools
import math

import jax
import jax.numpy as jnp
import numpy as np
from jax import lax
from jax.experimental import pallas as pl
from jax.experimental.pallas import tpu as pltpu

F32 = jnp.float32
BF16 = jnp.bfloat16
EPS = 1e-6

LANES = 128
SSM_L = 16
SGU_CHUNK = 128
VMEM_LIMIT = 60 * 1024 * 1024


def _cparams(sem):
    return pltpu.CompilerParams(dimension_semantics=sem, vmem_limit_bytes=VMEM_LIMIT)


def _const_spec(shape):
    nd = len(shape)
    return pl.BlockSpec(shape, lambda *_: (0,) * nd, pipeline_mode=pl.Buffered(1))


def _rms(xf, g):
    return xf * lax.rsqrt(jnp.mean(xf * xf, axis=-1, keepdims=True) + EPS) * g


def _gelu_tanh(x):
    k = math.sqrt(2.0 / math.pi)
    hx = 0.5 * x
    return hx + hx * jnp.tanh(x * (k + (k * 0.044715) * (x * x)))


def _ada_kernel(c_ref, w_ref, b_ref, o_ref):
    c = c_ref[...]
    ca = (c * jax.nn.sigmoid(c)).astype(BF16)
    o_ref[...] = jnp.dot(ca, w_ref[...].astype(BF16), preferred_element_type=F32) + b_ref[...]


def _ada(c, w, b, tn=1024):
    bsz, d = c.shape
    n = w.shape[1]
    return pl.pallas_call(
        _ada_kernel,
        grid=(n // tn,),
        in_specs=[pl.BlockSpec((bsz, d), lambda j: (0, 0)),
                  pl.BlockSpec((d, tn), lambda j: (0, j)),
                  pl.BlockSpec((1, tn), lambda j: (0, j))],
        out_specs=pl.BlockSpec((bsz, tn), lambda j: (0, j)),
        out_shape=jax.ShapeDtypeStruct((bsz, n), F32),
        compiler_params=_cparams(("arbitrary",)),
        name="ada",
    )(c, w, b.reshape(1, n))


INPROJ_PIECES = 2


def _inproj_kernel(x_ref, sc_ref, sh_ref, g_ref, w_ref, *rest, d_ssm, n_cast):
    cast_in = rest[:n_cast]
    zs_ref, zuv_ref = rest[n_cast:n_cast + 2]
    cast_out = rest[n_cast + 2:]
    gs = g_ref[...] * (1.0 + sc_ref[0])
    tm = x_ref.shape[1]
    piece = tm // INPROJ_PIECES
    for r0 in range(0, tm, piece):
        rows = slice(r0, r0 + piece)
        h = _rms(x_ref[0, rows], gs) + sh_ref[0]
        z = jnp.dot(h.astype(BF16), w_ref[...], preferred_element_type=F32)
        for j in range(d_ssm // LANES):
            zs_ref[0, j, rows] = z[:, j * LANES:(j + 1) * LANES]
        zuv_ref[0, rows] = z[:, d_ssm:].astype(zuv_ref.dtype)
        if r0 == 0:
            for src, dst in zip(cast_in, cast_out):
                dst[...] = src[...].astype(dst.dtype)


def _inproj(x, sc, sh, g, w_bf16, d_ssm, later_weights, tm=512):
    bsz, s, d = x.shape
    d_in = w_bf16.shape[1]
    nslab = d_ssm // LANES
    nt = s // tm
    steps = bsz * nt
    wspecs = [pl.BlockSpec((w.shape[0] // steps, w.shape[1]), lambda b, i: (b * nt + i, 0))
              for w in later_weights]
    return pl.pallas_call(
        functools.partial(_inproj_kernel, d_ssm=d_ssm, n_cast=len(later_weights)),
        grid=(bsz, nt),
        in_specs=[pl.BlockSpec((1, tm, d), lambda b, i: (b, i, 0)),
                  pl.BlockSpec((1, 1, d), lambda b, i: (b, 0, 0)),
                  pl.BlockSpec((1, 1, d), lambda b, i: (b, 0, 0)),
                  _const_spec((1, d)),
                  _const_spec((d, d_in))] + wspecs,
        out_specs=[pl.BlockSpec((1, nslab, tm, LANES), lambda b, i: (b, 0, i, 0)),
                   pl.BlockSpec((1, tm, d_in - d_ssm), lambda b, i: (b, i, 0))] + wspecs,
        out_shape=[jax.ShapeDtypeStruct((bsz, nslab, s, LANES), F32),
                   jax.ShapeDtypeStruct((bsz, s, d_in - d_ssm), BF16)]
                  + [jax.ShapeDtypeStruct(w.shape, BF16) for w in later_weights],
        compiler_params=_cparams(("arbitrary", "arbitrary")),
        name="inproj",
    )(x, sc, sh, g.reshape(1, d), w_bf16, *later_weights)


def _tables_kernel(lp_ref, lt_ref, bt_ref, ct_ref, d_ref, erep_ref, etile_ref, m_ref, p_ref, q_ref):
    hp = lax.Precision.HIGHEST
    gb = lp_ref.shape[0]
    L = lp_ref.shape[2] - 1
    n = lp_ref.shape[3]
    c = bt_ref.shape[2]
    w = L * c
    lane = lax.broadcasted_iota(jnp.int32, (c, w), 1)
    row = lax.broadcasted_iota(jnp.int32, (c, w), 0)

    def spread(x, e_ref):
        hi = x.astype(BF16)
        r = x - hi.astype(F32)
        mid = r.astype(BF16)
        lo = (r - mid.astype(F32)).astype(BF16)
        e = e_ref[...]
        return (jnp.dot(hi, e, preferred_element_type=F32) + jnp.dot(mid, e, preferred_element_type=F32)
                + jnp.dot(lo, e, preferred_element_type=F32))

    for g in range(gb):
        lq = [jnp.concatenate([jnp.broadcast_to(lp_ref[g, k, L - 1 - s:L - s, :], (c, n)) for s in range(L)],
                              axis=0) for k in range(2)]
        bt = [jnp.concatenate([bt_ref[g, k]] * L, axis=0) for k in range(2)]
        p_ref[g, :, :n] = (lq[0] * bt[0] - lq[1] * bt[1]).astype(p_ref.dtype)
        p_ref[g, :, n:] = (lq[0] * bt[1] + lq[1] * bt[0]).astype(p_ref.dtype)
        cx = [spread(ct_ref[g, k], etile_ref) for k in range(2)]
        lx = [spread(lt_ref[g, k], erep_ref) for k in range(2)]
        v0_re = cx[0] * lx[0] - cx[1] * lx[1]
        v0_im = cx[0] * lx[1] + cx[1] * lx[0]
        lam_re = lt_ref[g, 0][:, 1:2]
        lam_im = lt_ref[g, 1][:, 1:2]
        v1_re = v0_re * lam_re - v0_im * lam_im
        v1_im = v0_re * lam_im + v0_im * lam_re
        q_ref[g, :n, :] = v1_re.astype(q_ref.dtype)
        q_ref[g, n:, :] = (-v1_im).astype(q_ref.dtype)
        kc = (jnp.dot(bt_ref[g, 0], v0_re, precision=hp, preferred_element_type=F32)
              - jnp.dot(bt_ref[g, 1], v0_im, precision=hp, preferred_element_type=F32))
        kc = kc + jnp.where(lane == row, d_ref[g], 0.0)
        for s in range(L):
            blk = kc if s == 0 else jnp.where(lane >= s * c, pltpu.roll(kc, s * c, 1), 0.0)
            m_ref[g, s * c:(s + 1) * c, :] = blk.astype(m_ref.dtype)


def _tables(lp, bbt, ct, d, gb=8):
    g, _, lp1, n = lp.shape
    L = lp1 - 1
    c = bbt.shape[2]
    w = L * c
    lt = lp[:, :, :L].transpose(0, 1, 3, 2)
    col = np.arange(w)
    e_rep = jnp.asarray(col[None, :] // c == np.arange(L)[:, None], dtype=BF16)
    e_tile = jnp.asarray(col[None, :] % c == np.arange(c)[:, None], dtype=BF16)
    blk = lambda *s: pl.BlockSpec((gb,) + s, lambda i: (i,) + (0,) * len(s))
    return pl.pallas_call(
        _tables_kernel,
        grid=(g // gb,),
        in_specs=[blk(2, lp1, n), blk(2, n, L), blk(2, c, n), blk(2, n, c), blk(c, 1),
                  _const_spec((L, w)), _const_spec((c, w))],
        out_specs=[blk(w, w), blk(w, 2 * n), blk(2 * n, w)],
        out_shape=[jax.ShapeDtypeStruct((g, w, w), BF16), jax.ShapeDtypeStruct((g, w, 2 * n), BF16),
                   jax.ShapeDtypeStruct((g, 2 * n, w), BF16)],
        compiler_params=_cparams(("arbitrary",)),
        name="tables",
    )(lp, lt, bbt, ct, d, e_rep, e_tile)


def _ssm_tables(log_dt, a_re, a_im, b_re, b_im, c_re, c_im, d, w_glu):
    L = SSM_L
    G, N, C = b_re.shape
    dt = jnp.exp(log_dt)[:, None]
    ar_dt = a_re * dt
    ai_dt = a_im * dt
    mag = jnp.exp(ar_dt)
    lb_re = mag * jnp.cos(ai_dt)
    lb_im = mag * jnp.sin(ai_dt)
    den = a_re * a_re + a_im * a_im
    nr = lb_re - 1.0
    ni = lb_im
    f_re = ((nr * a_re + ni * a_im) / den)[:, :, None]
    f_im = ((ni * a_re - nr * a_im) / den)[:, :, None]
    bb_re = f_re * b_re - f_im * b_im
    bb_im = f_re * b_im + f_im * b_re
    tau = jnp.arange(L + 1, dtype=F32)[None, :, None]
    pmag = jnp.exp(ar_dt[:, None, :] * tau)
    lp_re = pmag * jnp.cos(ai_dt[:, None, :] * tau)
    lp_im = pmag * jnp.sin(ai_dt[:, None, :] * tau)
    m, p, q = _tables(jnp.stack([lp_re, lp_im], axis=1),
                      jnp.stack([bb_re, bb_im], axis=1).transpose(0, 1, 3, 2),
                      jnp.stack([c_re, c_im], axis=1).transpose(0, 1, 3, 2),
                      d.reshape(G, C, 1))
    l_re = lp_re[:, L]
    l_im = lp_im[:, L]
    a1 = jnp.concatenate([l_re, l_re], axis=-1)
    a2u = jnp.concatenate([-l_im, l_im], axis=-1)
    a2v = jnp.concatenate([l_im, -l_im], axis=-1)
    gps = 2 * LANES // C
    wg = w_glu.reshape(G // gps, gps, C, C)
    wg = (wg[:, :, :, None, :] * jnp.eye(gps, dtype=F32)[None, :, None, :, None]).reshape(
        G // gps, gps * C, gps * C)
    return m, p, q, a1, a2u, a2v, wg.astype(BF16)


def _block_swap_perm():
    idx = np.arange(8 * LANES)
    a, b, c = idx // LANES, (idx // SSM_L) % 8, idx % SSM_L
    return jnp.asarray(idx[None, :] == (b * LANES + a * SSM_L + c)[:, None], dtype=BF16)


def _ssm_kernel(zs_ref, perm_ref, m_ref, p_ref, q_ref, a1_ref, a2u_ref, a2v_ref, wg_ref, bg_ref, go_ref,
                o_ref, zf_ref, xu_ref, xv_ref, hp_ref, ys_ref, hu_ref, hv_ref, *, n, pitch, groups):
    L = SSM_L
    d = groups * L
    nslab = d // LANES

    @pl.when(pl.program_id(1) == 0)
    def _():
        hu_ref[...] = jnp.zeros_like(hu_ref)
        hv_ref[...] = jnp.zeros_like(hv_ref)

    def swap_blocks(pieces):
        lhs = jnp.concatenate([jnp.concatenate(ps, axis=1) for ps in pieces], axis=0)
        return jnp.dot(lhs, perm_ref[...], preferred_element_type=F32).astype(BF16)

    for hh in range(2):
        out = swap_blocks([[zs_ref[0, j, pl.ds(8 * hh + tp, n, stride=L), :].astype(BF16)
                            for tp in range(8)] for j in range(nslab)])
        for j in range(nslab):
            for qq in range(8):
                zf_ref[8 * j + qq, :, hh * LANES:(hh + 1) * LANES] = (
                    out[j * n:(j + 1) * n, qq * LANES:(qq + 1) * LANES])

    gx = 8

    def x_body(i, carry):
        for gi in range(gx):
            g = i * gx + gi
            xo = jnp.dot(zf_ref[g], p_ref[g], preferred_element_type=F32)
            r = pl.ds(pl.multiple_of(g * pitch, 8), n)
            xu_ref[r, :] = xo
            xv_ref[r, :] = pltpu.roll(xo, LANES // 2, 1)
        return carry

    lax.fori_loop(0, groups // gx, x_body, 0)

    a1 = a1_ref[...]
    a2u = a2u_ref[...]
    a2v = a2v_ref[...]
    ks = 8

    def scan_body(i, carry):
        hu, hv = carry
        for s in range(ks):
            kk = i * ks + s
            u = xu_ref[pl.ds(kk, groups, stride=pitch), :]
            v = xv_ref[pl.ds(kk, groups, stride=pitch), :]
            hp_ref[pl.ds(kk, groups, stride=pitch), :] = hu
            hu, hv = a1 * hu + a2u * hv + u, a1 * hv + a2v * hu + v
        return hu, hv

    hu, hv = lax.fori_loop(0, n // ks, scan_body, (hu_ref[...], hv_ref[...]))
    hu_ref[...] = hu
    hv_ref[...] = hv

    gy = 8

    def y_body(i, carry):
        for gi in range(gy):
            g = i * gy + gi
            hprev = hp_ref[pl.ds(pl.multiple_of(g * pitch, 8), n), :].astype(BF16)
            y = (jnp.dot(zf_ref[g], m_ref[g], preferred_element_type=F32)
                 + jnp.dot(hprev, q_ref[g], preferred_element_type=F32))
            zf_ref[g] = _gelu_tanh(y).astype(BF16)
        return carry

    lax.fori_loop(0, groups // gy, y_body, 0)

    for hh in range(2):
        out = swap_blocks([[zf_ref[8 * j + qq, :, hh * LANES:(hh + 1) * LANES] for qq in range(8)]
                           for j in range(nslab)])
        for j in range(nslab):
            for tp in range(8):
                ys_ref[:, (8 * hh + tp) * d + j * LANES:(8 * hh + tp) * d + (j + 1) * LANES] = (
                    out[j * n:(j + 1) * n, tp * LANES:(tp + 1) * LANES])

    w2 = 2 * LANES
    for t in range(L):
        yb = ys_ref[:, t * d:(t + 1) * d]
        gate = jnp.concatenate(
            [jnp.dot(yb[:, i * w2:(i + 1) * w2], wg_ref[i], preferred_element_type=F32)
             for i in range(d // w2)], axis=1) + bg_ref[...]
        y = _rms(yb.astype(F32) * jax.nn.sigmoid(gate), go_ref[...])
        for j in range(nslab):
            o_ref[0, j, pl.ds(t, n, stride=L), :] = y[:, j * LANES:(j + 1) * LANES]


def _ssm(zs, tables, b_glu, g_out, sb=1024):
    m, p, q, a1, a2u, a2v, wg = tables
    bsz, nslab, s, _ = zs.shape
    d_ssm = nslab * LANES
    groups = m.shape[0]
    L = SSM_L
    n = sb // L
    pitch = n + 8
    perm = _block_swap_perm()
    kern = functools.partial(_ssm_kernel, n=n, pitch=pitch, groups=groups)
    slabs = pl.BlockSpec((1, nslab, sb, LANES), lambda b, i: (b, 0, i, 0))
    return pl.pallas_call(
        kern,
        grid=(bsz, s // sb),
        in_specs=[slabs,
                  _const_spec(perm.shape),
                  _const_spec(m.shape), _const_spec(p.shape), _const_spec(q.shape),
                  _const_spec(a1.shape), _const_spec(a2u.shape), _const_spec(a2v.shape),
                  _const_spec(wg.shape), _const_spec((1, d_ssm)), _const_spec((1, d_ssm))],
        out_specs=slabs,
        out_shape=jax.ShapeDtypeStruct((bsz, nslab, s, LANES), F32),
        scratch_shapes=[pltpu.VMEM((groups, n, 2 * LANES), BF16),
                        pltpu.VMEM((groups * pitch, LANES), F32),
                        pltpu.VMEM((groups * pitch, LANES), F32),
                        pltpu.VMEM((groups * pitch, LANES), F32),
                        pltpu.VMEM((n, L * d_ssm), BF16),
                        pltpu.VMEM((groups, LANES), F32),
                        pltpu.VMEM((groups, LANES), F32)],
        compiler_params=_cparams(("arbitrary", "arbitrary")),
        name="ssm",
    )(zs, perm, m, p, q, a1, a2u, a2v, wg, b_glu.reshape(1, d_ssm), g_out.reshape(1, d_ssm))


def _sgu_kernel(zuv_ref, lng_ref, lnb_ref, w_ref, bs_ref, go_ref, o_ref, *, heads, hd):
    d = heads * hd
    ts = zuv_ref.shape[1]
    nc = ts // SGU_CHUNK
    u = _gelu_tanh(zuv_ref[0, :, :d].astype(F32))
    v = _gelu_tanh(zuv_ref[0, :, d:].astype(F32))
    mu = jnp.mean(v, axis=-1, keepdims=True)
    vc = v - mu
    v = vc * lax.rsqrt(jnp.mean(vc * vc, axis=-1, keepdims=True) + EPS) * lng_ref[...] + lnb_ref[...]
    vb = v.astype(BF16)
    cols = []
    for h in range(heads):
        vh = jnp.concatenate([vb[c * SGU_CHUNK:(c + 1) * SGU_CHUNK, h * hd:(h + 1) * hd]
                              for c in range(nc)], axis=1)
        mixed = jnp.dot(w_ref[h], vh, preferred_element_type=F32)
        mixed = jnp.concatenate([mixed[:, c * hd:(c + 1) * hd] for c in range(nc)], axis=0)
        cols.append(mixed)
    mixed = jnp.concatenate(cols, axis=1)
    bias = jnp.concatenate([bs_ref[...]] * nc, axis=0)
    y = u * (mixed + bias)
    o_ref[0] = _rms(y, go_ref[...]).astype(o_ref.dtype)


def _sgu(zuv, ln_g, ln_b, w_masked_bf16, b_full, g_out, heads, ts=512):
    bsz, s, d2 = zuv.shape
    d = d2 // 2
    hd = d // heads
    return pl.pallas_call(
        functools.partial(_sgu_kernel, heads=heads, hd=hd),
        grid=(bsz, s // ts),
        in_specs=[pl.BlockSpec((1, ts, d2), lambda b, i: (b, i, 0)),
                  _const_spec((1, d)), _const_spec((1, d)),
                  _const_spec(w_masked_bf16.shape), _const_spec(b_full.shape), _const_spec((1, d))],
        out_specs=pl.BlockSpec((1, ts, d), lambda b, i: (b, i, 0)),
        out_shape=jax.ShapeDtypeStruct((bsz, s, d), BF16),
        compiler_params=_cparams(("arbitrary", "arbitrary")),
        name="sgu",
    )(zuv, ln_g.reshape(1, d), ln_b.reshape(1, d), w_masked_bf16, b_full, g_out.reshape(1, d))


OUTPROJ_PIECES = 4


def _outproj_kernel(ya_ref, yb_ref, x_ref, gt_ref, g_ref, wa_ref, wb_ref, sc_ref, sh_ref, gn_ref,
                    o_ref, h_ref):
    piece = x_ref.shape[1] // OUTPROJ_PIECES
    gg = g_ref[...] * gt_ref[0]
    gs = gn_ref[...] * (1.0 + sc_ref[0])
    for r0 in range(0, x_ref.shape[1], piece):
        rows = slice(r0, r0 + piece)
        ya = jnp.concatenate([ya_ref[0, j, rows] for j in range(ya_ref.shape[1])], axis=1).astype(BF16)
        y = (jnp.dot(ya, wa_ref[...], preferred_element_type=F32)
             + jnp.dot(yb_ref[0, rows], wb_ref[...], preferred_element_type=F32))
        x1 = x_ref[0, rows] + _rms(y, gg)
        o_ref[0, rows] = x1
        h_ref[0, rows] = (_rms(x1, gs) + sh_ref[0]).astype(h_ref.dtype)


def _outproj(ya, yb, x, gt, g, w_bf16, sc_next, sh_next, g_next, tm=512):
    bsz, s, d = x.shape
    nslab = ya.shape[1]
    db = yb.shape[-1]
    assert nslab * LANES == db and w_bf16.shape[0] == 2 * db
    row = lambda w: pl.BlockSpec((1, tm, w), lambda b, i: (b, i, 0))
    per_batch = pl.BlockSpec((1, 1, d), lambda b, i: (b, 0, 0))
    half = lambda k: pl.BlockSpec((db, d), lambda b, i: (k, 0), pipeline_mode=pl.Buffered(1))
    return pl.pallas_call(
        _outproj_kernel,
        grid=(bsz, s // tm),
        in_specs=[pl.BlockSpec((1, nslab, tm, LANES), lambda b, i: (b, 0, i, 0)), row(db), row(d),
                  per_batch, _const_spec((1, d)), half(0), half(1),
                  per_batch, per_batch, _const_spec((1, d))],
        out_specs=[row(d), row(d)],
        out_shape=[jax.ShapeDtypeStruct((bsz, s, d), F32), jax.ShapeDtypeStruct((bsz, s, d), BF16)],
        compiler_params=_cparams(("arbitrary", "arbitrary")),
        name="outproj",
    )(ya, yb, x, gt, g.reshape(1, d), w_bf16, w_bf16, sc_next, sh_next, g_next.reshape(1, d))


FFN_HALO = 16
def _ffn_up_kernel(hin_ref, hh_ref, wa_ref, wb_ref, cwa_ref, cwb_ref, cba_ref, cbb_ref, o_ref, h_ref):
    i = pl.program_id(1)
    j = pl.program_id(2)

    @pl.when(j == 0)
    def _():
        halo = hh_ref[0]
        h_ref[:FFN_HALO, :] = jnp.where(i > 0, halo, jnp.zeros_like(halo))
        h_ref[FFN_HALO:, :] = hin_ref[0]

    h = h_ref[...]

    def branch(w_ref, cw_ref, cb_ref):
        up = jnp.dot(h, w_ref[...], preferred_element_type=F32)
        cw = cw_ref[...]
        r = (cw[0:1] * pltpu.roll(up, 2, 0) + cw[1:2] * pltpu.roll(up, 1, 0) + cw[2:3] * up)
        return r[FFN_HALO:] + cb_ref[...]

    a = branch(wa_ref, cwa_ref, cba_ref)
    b = branch(wb_ref, cwb_ref, cbb_ref)
    o_ref[0] = (a * jax.nn.sigmoid(a) * b).astype(o_ref.dtype)


def _ffn_up(h, w_up_bf16, conv_w, conv_b, tm=1024, tf=512):
    bsz, s, d = h.shape
    f = w_up_bf16.shape[1] // 2
    nf = f // tf
    hb = tm // FFN_HALO
    cb = conv_b.reshape(1, 2 * f)
    return pl.pallas_call(
        _ffn_up_kernel,
        grid=(bsz, s // tm, nf),
        in_specs=[pl.BlockSpec((1, tm, d), lambda b, i, j: (b, i, 0)),
                  pl.BlockSpec((1, FFN_HALO, d), lambda b, i, j: (b, jnp.maximum(i * hb - 1, 0), 0)),
                  pl.BlockSpec((d, tf), lambda b, i, j: (0, j)),
                  pl.BlockSpec((d, tf), lambda b, i, j: (0, nf + j)),
                  pl.BlockSpec((3, tf), lambda b, i, j: (0, j)),
                  pl.BlockSpec((3, tf), lambda b, i, j: (0, nf + j)),
                  pl.BlockSpec((1, tf), lambda b, i, j: (0, j)),
                  pl.BlockSpec((1, tf), lambda b, i, j: (0, nf + j))],
        out_specs=pl.BlockSpec((1, tm, tf), lambda b, i, j: (b, i, j)),
        out_shape=jax.ShapeDtypeStruct((bsz, s, f), BF16),
        scratch_shapes=[pltpu.VMEM((FFN_HALO + tm, d), BF16)],
        compiler_params=_cparams(("arbitrary", "arbitrary", "arbitrary")),
        name="ffn_up",
    )(h, h, w_up_bf16, w_up_bf16, conv_w, conv_w, cb, cb)


FFN_DOWN_PIECES = 2


def _ffn_down_kernel(act_ref, x_ref, gt_ref, g_ref, wd_ref, o_ref):
    gg = g_ref[...] * gt_ref[0]
    piece = x_ref.shape[1] // FFN_DOWN_PIECES
    for r0 in range(0, x_ref.shape[1], piece):
        rows = slice(r0, r0 + piece)
        f = jnp.dot(act_ref[0, rows], wd_ref[...], preferred_element_type=F32)
        o_ref[0, rows] = x_ref[0, rows] + _rms(f, gg)


def _ffn_down(act, x, gt, g_post, w_down_bf16, tm=512):
    bsz, s, d = x.shape
    f = act.shape[-1]
    return pl.pallas_call(
        _ffn_down_kernel,
        grid=(bsz, s // tm),
        in_specs=[pl.BlockSpec((1, tm, f), lambda b, i: (b, i, 0)),
                  pl.BlockSpec((1, tm, d), lambda b, i: (b, i, 0)),
                  pl.BlockSpec((1, 1, d), lambda b, i: (b, 0, 0)),
                  _const_spec((1, d)), _const_spec((f, d))],
        out_specs=pl.BlockSpec((1, tm, d), lambda b, i: (b, i, 0)),
        out_shape=jax.ShapeDtypeStruct((bsz, s, d), F32),
        compiler_params=_cparams(("arbitrary", "arbitrary")),
        name="ffn_down",
    )(act, x, gt, g_post.reshape(1, d), w_down_bf16)


def kernel(x, c, w_ada, b_ada, g_pre_mix, g_post_mix, w_in, ssm_log_dt, ssm_a_re, ssm_a_im, ssm_b_re, ssm_b_im, ssm_c_re, ssm_c_im, ssm_d, ssm_w_glu, ssm_b_glu, sgu_ln_g, sgu_ln_b, sgu_w, sgu_b, g_out_ssm, g_out_sgu, w_out, g_pre_ffn, g_post_ffn, w_up, conv_w, conv_b, w_down):
    depth = w_ada.shape[0]
    d = x.shape[-1]
    d_ssm = ssm_d.shape[-1]
    heads = sgu_w.shape[1]
    hd = (w_out.shape[1] - d_ssm) // heads
    for l in range(depth):
        mod = _ada(c, w_ada[l], b_ada[l])
        sh1, sc1, gt1, sh2, sc2, gt2 = [mod[:, None, k * d:(k + 1) * d] for k in range(6)]

        zs, zuv, wo, wu, wd = _inproj(x, sc1, sh1, g_pre_mix[l], w_in[l].astype(BF16), d_ssm,
                                      (w_out[l], w_up[l], w_down[l]))
        tables = _ssm_tables(ssm_log_dt[l], ssm_a_re[l], ssm_a_im[l], ssm_b_re[l], ssm_b_im[l],
                             ssm_c_re[l], ssm_c_im[l], ssm_d[l], ssm_w_glu[l])
        y_ssm = _ssm(zs, tables, ssm_b_glu[l], g_out_ssm[l])
        w_s = jnp.tril(sgu_w[l]).astype(BF16)
        b_full = jnp.repeat(sgu_b[l].T, hd, axis=1)
        y_sgu = _sgu(zuv, sgu_ln_g[l], sgu_ln_b[l], w_s, b_full, g_out_sgu[l], heads)
        x, h = _outproj(y_ssm, y_sgu, x, gt1, g_post_mix[l], wo, sc2, sh2, g_pre_ffn[l])

        act = _ffn_up(h, wu, conv_w[l], conv_b[l])
        x = _ffn_down(act, x, gt2, g_post_ffn[l], wd)
    return x
```

```python
import functools
import math

import jax
import jax.numpy as jnp
import numpy as np
from jax import lax
from jax.experimental import pallas as pl
from jax.experimental.pallas import tpu as pltpu

F32 = jnp.float32
BF16 = jnp.bfloat16
EPS = 1e-6

LANES = 128
SSM_L = 16
SGU_CHUNK = 128
VMEM_LIMIT = 60 * 1024 * 1024


def _cparams(sem):
    return pltpu.CompilerParams(dimension_semantics=sem, vmem_limit_bytes=VMEM_LIMIT)


def _const_spec(shape):
    nd = len(shape)
    return pl.BlockSpec(shape, lambda *_: (0,) * nd, pipeline_mode=pl.Buffered(1))


def _rms(xf, g):
    return xf * lax.rsqrt(jnp.mean(xf * xf, axis=-1, keepdims=True) + EPS) * g


def _gelu_tanh(x):
    k = math.sqrt(2.0 / math.pi)
    hx = 0.5 * x
    return hx + hx * jnp.tanh(x * (k + (k * 0.044715) * (x * x)))


def _ada_kernel(c_ref, w_ref, b_ref, o_ref):
    c = c_ref[...]
    ca = (c * jax.nn.sigmoid(c)).astype(BF16)
    o_ref[...] = jnp.dot(ca, w_ref[...].astype(BF16), preferred_element_type=F32) + b_ref[...]


def _ada(c, w, b, tn=1024):
    bsz, d = c.shape
    n = w.shape[1]
    return pl.pallas_call(
        _ada_kernel,
        grid=(n // tn,),
        in_specs=[pl.BlockSpec((bsz, d), lambda j: (0, 0)),
                  pl.BlockSpec((d, tn), lambda j: (0, j)),
                  pl.BlockSpec((1, tn), lambda j: (0, j))],
        out_specs=pl.BlockSpec((bsz, tn), lambda j: (0, j)),
        out_shape=jax.ShapeDtypeStruct((bsz, n), F32),
        compiler_params=_cparams(("arbitrary",)),
        name="ada",
    )(c, w, b.reshape(1, n))


INPROJ_PIECES = 2


def _inproj_kernel(x_ref, sc_ref, sh_ref, g_ref, w_ref, *rest, d_ssm, n_cast):
    cast_in = rest[:n_cast]
    zs_ref, zuv_ref = rest[n_cast:n_cast + 2]
    cast_out = rest[n_cast + 2:]
    gs = g_ref[...] * (1.0 + sc_ref[0])
    tm = x_ref.shape[1]
    piece = tm // INPROJ_PIECES
    for r0 in range(0, tm, piece):
        rows = slice(r0, r0 + piece)
        h = _rms(x_ref[0, rows], gs) + sh_ref[0]
        z = jnp.dot(h.astype(BF16), w_ref[...], preferred_element_type=F32)
        for j in range(d_ssm // LANES):
            zs_ref[0, j, rows] = z[:, j * LANES:(j + 1) * LANES]
        zuv_ref[0, rows] = z[:, d_ssm:].astype(zuv_ref.dtype)
        if r0 == 0:
            for src, dst in zip(cast_in, cast_out):
                dst[...] = src[...].astype(dst.dtype)


def _inproj(x, sc, sh, g, w_bf16, d_ssm, later_weights, tm=512):
    bsz, s, d = x.shape
    d_in = w_bf16.shape[1]
    nslab = d_ssm // LANES
    nt = s // tm
    steps = bsz * nt
    wspecs = [pl.BlockSpec((w.shape[0] // steps, w.shape[1]), lambda b, i: (b * nt + i, 0))
              for w in later_weights]
    return pl.pallas_call(
        functools.partial(_inproj_kernel, d_ssm=d_ssm, n_cast=len(later_weights)),
        grid=(bsz, nt),
        in_specs=[pl.BlockSpec((1, tm, d), lambda b, i: (b, i, 0)),
                  pl.BlockSpec((1, 1, d), lambda b, i: (b, 0, 0)),
                  pl.BlockSpec((1, 1, d), lambda b, i: (b, 0, 0)),
                  _const_spec((1, d)),
                  _const_spec((d, d_in))] + wspecs,
        out_specs=[pl.BlockSpec((1, nslab, tm, LANES), lambda b, i: (b, 0, i, 0)),
                   pl.BlockSpec((1, tm, d_in - d_ssm), lambda b, i: (b, i, 0))] + wspecs,
        out_shape=[jax.ShapeDtypeStruct((bsz, nslab, s, LANES), F32),
                   jax.ShapeDtypeStruct((bsz, s, d_in - d_ssm), BF16)]
                  + [jax.ShapeDtypeStruct(w.shape, BF16) for w in later_weights],
        compiler_params=_cparams(("arbitrary", "arbitrary")),
        name="inproj",
    )(x, sc, sh, g.reshape(1, d), w_bf16, *later_weights)


def _tables_kernel(lp_ref, lt_ref, bt_ref, ct_ref, d_ref, erep_ref, etile_ref, w_ref,
                   m_ref, p_ref, q_ref, wb_ref):
    wb_ref[...] = w_ref[...].astype(wb_ref.dtype)
    hp = lax.Precision.HIGHEST
    gb = lp_ref.shape[0]
    L = lp_ref.shape[2] - 1
    n = lp_ref.shape[3]
    c = bt_ref.shape[2]
    w = L * c
    lane = lax.broadcasted_iota(jnp.int32, (c, w), 1)
    row = lax.broadcasted_iota(jnp.int32, (c, w), 0)

    def spread(x, e_ref):
        hi = x.astype(BF16)
        r = x - hi.astype(F32)
        mid = r.astype(BF16)
        lo = (r - mid.astype(F32)).astype(BF16)
        e = e_ref[...]
        return (jnp.dot(hi, e, preferred_element_type=F32) + jnp.dot(mid, e, preferred_element_type=F32)
                + jnp.dot(lo, e, preferred_element_type=F32))

    for g in range(gb):
        lq = [jnp.concatenate([jnp.broadcast_to(lp_ref[g, k, L - 1 - s:L - s, :], (c, n)) for s in range(L)],
                              axis=0) for k in range(2)]
        bt = [jnp.concatenate([bt_ref[g, k]] * L, axis=0) for k in range(2)]
        p_ref[g, :, :n] = (lq[0] * bt[0] - lq[1] * bt[1]).astype(p_ref.dtype)
        p_ref[g, :, n:] = (lq[0] * bt[1] + lq[1] * bt[0]).astype(p_ref.dtype)
        cx = [spread(ct_ref[g, k], etile_ref) for k in range(2)]
        lx = [spread(lt_ref[g, k], erep_ref) for k in range(2)]
        v0_re = cx[0] * lx[0] - cx[1] * lx[1]
        v0_im = cx[0] * lx[1] + cx[1] * lx[0]
        lam_re = lt_ref[g, 0][:, 1:2]
        lam_im = lt_ref[g, 1][:, 1:2]
        v1_re = v0_re * lam_re - v0_im * lam_im
        v1_im = v0_re * lam_im + v0_im * lam_re
        q_ref[g, :n, :] = v1_re.astype(q_ref.dtype)
        q_ref[g, n:, :] = (-v1_im).astype(q_ref.dtype)
        kc = (jnp.dot(bt_ref[g, 0], v0_re, precision=hp, preferred_element_type=F32)
              - jnp.dot(bt_ref[g, 1], v0_im, precision=hp, preferred_element_type=F32))
        kc = kc + jnp.where(lane == row, d_ref[g], 0.0)
        for s in range(L):
            blk = kc if s == 0 else jnp.where(lane >= s * c, pltpu.roll(kc, s * c, 1), 0.0)
            m_ref[g, s * c:(s + 1) * c, :] = blk.astype(m_ref.dtype)


def _tables(lp, bbt, ct, d, w_in, gb=8):
    g, _, lp1, n = lp.shape
    wrows = w_in.shape[0] // (g // gb)
    L = lp1 - 1
    c = bbt.shape[2]
    w = L * c
    lt = lp[:, :, :L].transpose(0, 1, 3, 2)
    col = np.arange(w)
    e_rep = jnp.asarray(col[None, :] // c == np.arange(L)[:, None], dtype=BF16)
    e_tile = jnp.asarray(col[None, :] % c == np.arange(c)[:, None], dtype=BF16)
    blk = lambda *s: pl.BlockSpec((gb,) + s, lambda i: (i,) + (0,) * len(s))
    return pl.pallas_call(
        _tables_kernel,
        grid=(g // gb,),
        in_specs=[blk(2, lp1, n), blk(2, n, L), blk(2, c, n), blk(2, n, c), blk(c, 1),
                  _const_spec((L, w)), _const_spec((c, w)),
                  pl.BlockSpec((wrows, w_in.shape[1]), lambda i: (i, 0))],
        out_specs=[blk(w, w), blk(w, 2 * n), blk(2 * n, w),
                   pl.BlockSpec((wrows, w_in.shape[1]), lambda i: (i, 0))],
        out_shape=[jax.ShapeDtypeStruct((g, w, w), BF16), jax.ShapeDtypeStruct((g, w, 2 * n), BF16),
                   jax.ShapeDtypeStruct((g, 2 * n, w), BF16), jax.ShapeDtypeStruct(w_in.shape, BF16)],
        compiler_params=_cparams(("arbitrary",)),
        name="tables",
    )(lp, lt, bbt, ct, d, e_rep, e_tile, w_in)


def _ssm_tables(log_dt, a_re, a_im, b_re, b_im, c_re, c_im, d, w_glu, w_in):
    L = SSM_L
    G, N, C = b_re.shape
    dt = jnp.exp(log_dt)[:, None]
    ar_dt = a_re * dt
    ai_dt = a_im * dt
    mag = jnp.exp(ar_dt)
    lb_re = mag * jnp.cos(ai_dt)
    lb_im = mag * jnp.sin(ai_dt)
    den = a_re * a_re + a_im * a_im
    nr = lb_re - 1.0
    ni = lb_im
    f_re = ((nr * a_re + ni * a_im) / den)[:, :, None]
    f_im = ((ni * a_re - nr * a_im) / den)[:, :, None]
    bb_re = f_re * b_re - f_im * b_im
    bb_im = f_re * b_im + f_im * b_re
    tau = jnp.arange(L + 1, dtype=F32)[None, :, None]
    pmag = jnp.exp(ar_dt[:, None, :] * tau)
    lp_re = pmag * jnp.cos(ai_dt[:, None, :] * tau)
    lp_im = pmag * jnp.sin(ai_dt[:, None, :] * tau)
    m, p, q, w_in_bf16 = _tables(jnp.stack([lp_re, lp_im], axis=1),
                                 jnp.stack([bb_re, bb_im], axis=1).transpose(0, 1, 3, 2),
                                 jnp.stack([c_re, c_im], axis=1).transpose(0, 1, 3, 2),
                                 d.reshape(G, C, 1), w_in)
    l_re = lp_re[:, L]
    l_im = lp_im[:, L]
    a1 = jnp.concatenate([l_re, l_re], axis=-1)
    a2u = jnp.concatenate([-l_im, l_im], axis=-1)
    a2v = jnp.concatenate([l_im, -l_im], axis=-1)
    gps = 2 * LANES // C
    wg = w_glu.reshape(G // gps, gps, C, C)
    wg = (wg[:, :, :, None, :] * jnp.eye(gps, dtype=F32)[None, :, None, :, None]).reshape(
        G // gps, gps * C, gps * C)
    return (m, p, q, a1, a2u, a2v, wg.astype(BF16)), w_in_bf16


def _block_swap_perm():
    idx = np.arange(8 * LANES)
    a, b, c = idx // LANES, (idx // SSM_L) % 8, idx % SSM_L
    return jnp.asarray(idx[None, :] == (b * LANES + a * SSM_L + c)[:, None], dtype=BF16)


def _ssm_kernel(zs_ref, perm_ref, m_ref, p_ref, q_ref, a1_ref, a2u_ref, a2v_ref, wg_ref, bg_ref, go_ref,
                o_ref, zf_ref, xu_ref, xv_ref, hp_ref, ys_ref, hu_ref, hv_ref, *, n, pitch, groups):
    L = SSM_L
    d = groups * L
    nslab = d // LANES

    @pl.when(pl.program_id(1) == 0)
    def _():
        hu_ref[...] = jnp.zeros_like(hu_ref)
        hv_ref[...] = jnp.zeros_like(hv_ref)

    def swap_blocks(pieces):
        lhs = jnp.concatenate([jnp.concatenate(ps, axis=1) for ps in pieces], axis=0)
        return jnp.dot(lhs, perm_ref[...], preferred_element_type=F32).astype(BF16)

    for hh in range(2):
        out = swap_blocks([[zs_ref[0, j, pl.ds(8 * hh + tp, n, stride=L), :].astype(BF16)
                            for tp in range(8)] for j in range(nslab)])
        for j in range(nslab):
            for qq in range(8):
                zf_ref[8 * j + qq, :, hh * LANES:(hh + 1) * LANES] = (
                    out[j * n:(j + 1) * n, qq * LANES:(qq + 1) * LANES])

    gx = 8

    def x_body(i, carry):
        for gi in range(gx):
            g = i * gx + gi
            xo = jnp.dot(zf_ref[g], p_ref[g], preferred_element_type=F32)
            r = pl.ds(pl.multiple_of(g * pitch, 8), n)
            xu_ref[r, :] = xo
            xv_ref[r, :] = pltpu.roll(xo, LANES // 2, 1)
        return carry

    lax.fori_loop(0, groups // gx, x_body, 0)

    a1 = a1_ref[...]
    a2u = a2u_ref[...]
    a2v = a2v_ref[...]
    ks = 8

    def scan_body(i, carry):
        hu, hv = carry
        for s in range(ks):
            kk = i * ks + s
            u = xu_ref[pl.ds(kk, groups, stride=pitch), :]
            v = xv_ref[pl.ds(kk, groups, stride=pitch), :]
            hp_ref[pl.ds(kk, groups, stride=pitch), :] = hu
            hu, hv = a1 * hu + a2u * hv + u, a1 * hv + a2v * hu + v
        return hu, hv

    hu, hv = lax.fori_loop(0, n // ks, scan_body, (hu_ref[...], hv_ref[...]))
    hu_ref[...] = hu
    hv_ref[...] = hv

    gy = 8

    def y_body(i, carry):
        for gi in range(gy):
            g = i * gy + gi
            hprev = hp_ref[pl.ds(pl.multiple_of(g * pitch, 8), n), :].astype(BF16)
            y = (jnp.dot(zf_ref[g], m_ref[g], preferred_element_type=F32)
                 + jnp.dot(hprev, q_ref[g], preferred_element_type=F32))
            zf_ref[g] = _gelu_tanh(y).astype(BF16)
        return carry

    lax.fori_loop(0, groups // gy, y_body, 0)

    for hh in range(2):
        out = swap_blocks([[zf_ref[8 * j + qq, :, hh * LANES:(hh + 1) * LANES] for qq in range(8)]
                           for j in range(nslab)])
        for j in range(nslab):
            for tp in range(8):
                ys_ref[:, (8 * hh + tp) * d + j * LANES:(8 * hh + tp) * d + (j + 1) * LANES] = (
                    out[j * n:(j + 1) * n, tp * LANES:(tp + 1) * LANES])

    w2 = 2 * LANES
    for t in range(L):
        yb = ys_ref[:, t * d:(t + 1) * d]
        gate = jnp.concatenate(
            [jnp.dot(yb[:, i * w2:(i + 1) * w2], wg_ref[i], preferred_element_type=F32)
             for i in range(d // w2)], axis=1) + bg_ref[...]
        y = _rms(yb.astype(F32) * jax.nn.sigmoid(gate), go_ref[...])
        for j in range(nslab):
            o_ref[0, j, pl.ds(t, n, stride=L), :] = y[:, j * LANES:(j + 1) * LANES]


def _ssm(zs, tables, b_glu, g_out, sb=1024):
    m, p, q, a1, a2u, a2v, wg = tables
    bsz, nslab, s, _ = zs.shape
    d_ssm = nslab * LANES
    groups = m.shape[0]
    L = SSM_L
    n = sb // L
    pitch = n + 8
    perm = _block_swap_perm()
    kern = functools.partial(_ssm_kernel, n=n, pitch=pitch, groups=groups)
    slabs = pl.BlockSpec((1, nslab, sb, LANES), lambda b, i: (b, 0, i, 0))
    return pl.pallas_call(
        kern,
        grid=(bsz, s // sb),
        in_specs=[slabs,
                  _const_spec(perm.shape),
                  _const_spec(m.shape), _const_spec(p.shape), _const_spec(q.shape),
                  _const_spec(a1.shape), _const_spec(a2u.shape), _const_spec(a2v.shape),
                  _const_spec(wg.shape), _const_spec((1, d_ssm)), _const_spec((1, d_ssm))],
        out_specs=slabs,
        out_shape=jax.ShapeDtypeStruct((bsz, nslab, s, LANES), F32),
        scratch_shapes=[pltpu.VMEM((groups, n, 2 * LANES), BF16),
                        pltpu.VMEM((groups * pitch, LANES), F32),
                        pltpu.VMEM((groups * pitch, LANES), F32),
                        pltpu.VMEM((groups * pitch, LANES), F32),
                        pltpu.VMEM((n, L * d_ssm), BF16),
                        pltpu.VMEM((groups, LANES), F32),
                        pltpu.VMEM((groups, LANES), F32)],
        compiler_params=_cparams(("arbitrary", "arbitrary")),
        name="ssm",
    )(zs, perm, m, p, q, a1, a2u, a2v, wg, b_glu.reshape(1, d_ssm), g_out.reshape(1, d_ssm))


def _sgu_kernel(zuv_ref, lng_ref, lnb_ref, w_ref, bs_ref, go_ref, o_ref, *, heads, hd):
    d = heads * hd
    ts = zuv_ref.shape[1]
    nc = ts // SGU_CHUNK
    u = _gelu_tanh(zuv_ref[0, :, :d].astype(F32))
    v = _gelu_tanh(zuv_ref[0, :, d:].astype(F32))
    mu = jnp.mean(v, axis=-1, keepdims=True)
    vc = v - mu
    v = vc * lax.rsqrt(jnp.mean(vc * vc, axis=-1, keepdims=True) + EPS) * lng_ref[...] + lnb_ref[...]
    vb = v.astype(BF16)
    cols = []
    for h in range(heads):
        vh = jnp.concatenate([vb[c * SGU_CHUNK:(c + 1) * SGU_CHUNK, h * hd:(h + 1) * hd]
                              for c in range(nc)], axis=1)
        mixed = jnp.dot(w_ref[h], vh, preferred_element_type=F32)
        mixed = jnp.concatenate([mixed[:, c * hd:(c + 1) * hd] for c in range(nc)], axis=0)
        cols.append(mixed)
    mixed = jnp.concatenate(cols, axis=1)
    bias = jnp.concatenate([bs_ref[...]] * nc, axis=0)
    y = u * (mixed + bias)
    o_ref[0] = _rms(y, go_ref[...]).astype(o_ref.dtype)


def _sgu(zuv, ln_g, ln_b, w_masked_bf16, b_full, g_out, heads, ts=512):
    bsz, s, d2 = zuv.shape
    d = d2 // 2
    hd = d // heads
    return pl.pallas_call(
        functools.partial(_sgu_kernel, heads=heads, hd=hd),
        grid=(bsz, s // ts),
        in_specs=[pl.BlockSpec((1, ts, d2), lambda b, i: (b, i, 0)),
                  _const_spec((1, d)), _const_spec((1, d)),
                  _const_spec(w_masked_bf16.shape), _const_spec(b_full.shape), _const_spec((1, d))],
        out_specs=pl.BlockSpec((1, ts, d), lambda b, i: (b, i, 0)),
        out_shape=jax.ShapeDtypeStruct((bsz, s, d), BF16),
        compiler_params=_cparams(("arbitrary", "arbitrary")),
        name="sgu",
    )(zuv, ln_g.reshape(1, d), ln_b.reshape(1, d), w_masked_bf16, b_full, g_out.reshape(1, d))


OUTPROJ_PIECES = 4


def _outproj_kernel(ya_ref, yb_ref, x_ref, gt_ref, g_ref, wa_ref, wb_ref, sc_ref, sh_ref, gn_ref,
                    o_ref, h_ref):
    piece = x_ref.shape[1] // OUTPROJ_PIECES
    gg = g_ref[...] * gt_ref[0]
    gs = gn_ref[...] * (1.0 + sc_ref[0])
    for r0 in range(0, x_ref.shape[1], piece):
        rows = slice(r0, r0 + piece)
        ya = jnp.concatenate([ya_ref[0, j, rows] for j in range(ya_ref.shape[1])], axis=1).astype(BF16)
        y = (jnp.dot(ya, wa_ref[...], preferred_element_type=F32)
             + jnp.dot(yb_ref[0, rows], wb_ref[...], preferred_element_type=F32))
        x1 = x_ref[0, rows] + _rms(y, gg)
        o_ref[0, rows] = x1
        h_ref[0, rows] = (_rms(x1, gs) + sh_ref[0]).astype(h_ref.dtype)


def _outproj(ya, yb, x, gt, g, w_bf16, sc_next, sh_next, g_next, tm=512):
    bsz, s, d = x.shape
    nslab = ya.shape[1]
    db = yb.shape[-1]
    assert nslab * LANES == db and w_bf16.shape[0] == 2 * db
    row = lambda w: pl.BlockSpec((1, tm, w), lambda b, i: (b, i, 0))
    per_batch = pl.BlockSpec((1, 1, d), lambda b, i: (b, 0, 0))
    half = lambda k: pl.BlockSpec((db, d), lambda b, i: (k, 0), pipeline_mode=pl.Buffered(1))
    return pl.pallas_call(
        _outproj_kernel,
        grid=(bsz, s // tm),
        in_specs=[pl.BlockSpec((1, nslab, tm, LANES), lambda b, i: (b, 0, i, 0)), row(db), row(d),
                  per_batch, _const_spec((1, d)), half(0), half(1),
                  per_batch, per_batch, _const_spec((1, d))],
        out_specs=[row(d), row(d)],
        out_shape=[jax.ShapeDtypeStruct((bsz, s, d), F32), jax.ShapeDtypeStruct((bsz, s, d), BF16)],
        compiler_params=_cparams(("arbitrary", "arbitrary")),
        name="outproj",
    )(ya, yb, x, gt, g.reshape(1, d), w_bf16, w_bf16, sc_next, sh_next, g_next.reshape(1, d))


FFN_HALO = 16
def _ffn_up_kernel(hin_ref, hh_ref, wa_ref, wb_ref, cwa_ref, cwb_ref, cba_ref, cbb_ref, o_ref, h_ref):
    i = pl.program_id(1)
    j = pl.program_id(2)

    @pl.when(j == 0)
    def _():
        halo = hh_ref[0]
        h_ref[:FFN_HALO, :] = jnp.where(i > 0, halo, jnp.zeros_like(halo))
        h_ref[FFN_HALO:, :] = hin_ref[0]

    h = h_ref[...]

    def branch(w_ref, cw_ref, cb_ref):
        up = jnp.dot(h, w_ref[...], preferred_element_type=F32)
        cw = cw_ref[...]
        r = (cw[0:1] * pltpu.roll(up, 2, 0) + cw[1:2] * pltpu.roll(up, 1, 0) + cw[2:3] * up)
        return r[FFN_HALO:] + cb_ref[...]

    a = branch(wa_ref, cwa_ref, cba_ref)
    b = branch(wb_ref, cwb_ref, cbb_ref)
    o_ref[0] = (a * jax.nn.sigmoid(a) * b).astype(o_ref.dtype)


def _ffn_up(h, w_up_bf16, conv_w, conv_b, tm=1024, tf=512):
    bsz, s, d = h.shape
    f = w_up_bf16.shape[1] // 2
    nf = f // tf
    hb = tm // FFN_HALO
    cb = conv_b.reshape(1, 2 * f)
    return pl.pallas_call(
        _ffn_up_kernel,
        grid=(bsz, s // tm, nf),
        in_specs=[pl.BlockSpec((1, tm, d), lambda b, i, j: (b, i, 0)),
                  pl.BlockSpec((1, FFN_HALO, d), lambda b, i, j: (b, jnp.maximum(i * hb - 1, 0), 0)),
                  pl.BlockSpec((d, tf), lambda b, i, j: (0, j)),
                  pl.BlockSpec((d, tf), lambda b, i, j: (0, nf + j)),
                  pl.BlockSpec((3, tf), lambda b, i, j: (0, j)),
                  pl.BlockSpec((3, tf), lambda b, i, j: (0, nf + j)),
                  pl.BlockSpec((1, tf), lambda b, i, j: (0, j)),
                  pl.BlockSpec((1, tf), lambda b, i, j: (0, nf + j))],
        out_specs=pl.BlockSpec((1, tm, tf), lambda b, i, j: (b, i, j)),
        out_shape=jax.ShapeDtypeStruct((bsz, s, f), BF16),
        scratch_shapes=[pltpu.VMEM((FFN_HALO + tm, d), BF16)],
        compiler_params=_cparams(("arbitrary", "arbitrary", "arbitrary")),
        name="ffn_up",
    )(h, h, w_up_bf16, w_up_bf16, conv_w, conv_w, cb, cb)


FFN_DOWN_PIECES = 2


def _ffn_down_kernel(act_ref, x_ref, gt_ref, g_ref, wd_ref, o_ref):
    gg = g_ref[...] * gt_ref[0]
    piece = x_ref.shape[1] // FFN_DOWN_PIECES
    for r0 in range(0, x_ref.shape[1], piece):
        rows = slice(r0, r0 + piece)
        f = jnp.dot(act_ref[0, rows], wd_ref[...], preferred_element_type=F32)
        o_ref[0, rows] = x_ref[0, rows] + _rms(f, gg)


def _ffn_down(act, x, gt, g_post, w_down_bf16, tm=512):
    bsz, s, d = x.shape
    f = act.shape[-1]
    return pl.pallas_call(
        _ffn_down_kernel,
        grid=(bsz, s // tm),
        in_specs=[pl.BlockSpec((1, tm, f), lambda b, i: (b, i, 0)),
                  pl.BlockSpec((1, tm, d), lambda b, i: (b, i, 0)),
                  pl.BlockSpec((1, 1, d), lambda b, i: (b, 0, 0)),
                  _const_spec((1, d)), _const_spec((f, d))],
        out_specs=pl.BlockSpec((1, tm, d), lambda b, i: (b, i, 0)),
        out_shape=jax.ShapeDtypeStruct((bsz, s, d), F32),
        compiler_params=_cparams(("arbitrary", "arbitrary")),
        name="ffn_down",
    )(act, x, gt, g_post.reshape(1, d), w_down_bf16)


def kernel(x, c, w_ada, b_ada, g_pre_mix, g_post_mix, w_in, ssm_log_dt, ssm_a_re, ssm_a_im, ssm_b_re, ssm_b_im, ssm_c_re, ssm_c_im, ssm_d, ssm_w_glu, ssm_b_glu, sgu_ln_g, sgu_ln_b, sgu_w, sgu_b, g_out_ssm, g_out_sgu, w_out, g_pre_ffn, g_post_ffn, w_up, conv_w, conv_b, w_down):
    depth = w_ada.shape[0]
    d = x.shape[-1]
    d_ssm = ssm_d.shape[-1]
    heads = sgu_w.shape[1]
    hd = (w_out.shape[1] - d_ssm) // heads
    for l in range(depth):
        mod = _ada(c, w_ada[l], b_ada[l])
        sh1, sc1, gt1, sh2, sc2, gt2 = [mod[:, None, k * d:(k + 1) * d] for k in range(6)]

        tables, wi = _ssm_tables(ssm_log_dt[l], ssm_a_re[l], ssm_a_im[l], ssm_b_re[l], ssm_b_im[l],
                                 ssm_c_re[l], ssm_c_im[l], ssm_d[l], ssm_w_glu[l], w_in[l])
        zs, zuv, wo, wu, wd = _inproj(x, sc1, sh1, g_pre_mix[l], wi, d_ssm,
                                      (w_out[l], w_up[l], w_down[l]))
        y_ssm = _ssm(zs, tables, ssm_b_glu[l], g_out_ssm[l])
        w_s = jnp.tril(sgu_w[l]).astype(BF16)
        b_full = jnp.repeat(sgu_b[l].T, hd, axis=1)
        y_sgu = _sgu(zuv, sgu_ln_g[l], sgu_ln_b[l], w_s, b_full, g_out_sgu[l], heads)
        x, h = _outproj(y_ssm, y_sgu, x, gt1, g_post_mix[l], wo, sc2, sh2, g_pre_ffn[l])

        act = _ffn_up(h, wu, conv_w[l], conv_b[l])
        x = _ffn_down(act, x, gt2, g_post_ffn[l], wd)
    return x
```

```python
import functools
import math

import jax
import jax.numpy as jnp
import numpy as np
from jax import lax
from jax.experimental import pallas as pl
from jax.experimental.pallas import tpu as pltpu

F32 = jnp.float32
BF16 = jnp.bfloat16
EPS = 1e-6

LANES = 128
SSM_L = 16
SGU_CHUNK = 128
VMEM_LIMIT = 60 * 1024 * 1024


def _cparams(sem):
    return pltpu.CompilerParams(dimension_semantics=sem, vmem_limit_bytes=VMEM_LIMIT)


def _const_spec(shape):
    nd = len(shape)
    return pl.BlockSpec(shape, lambda *_: (0,) * nd, pipeline_mode=pl.Buffered(1))


def _rms(xf, g):
    return xf * lax.rsqrt(jnp.mean(xf * xf, axis=-1, keepdims=True) + EPS) * g


def _gelu_tanh(x):
    k = math.sqrt(2.0 / math.pi)
    hx = 0.5 * x
    return hx + hx * jnp.tanh(x * (k + (k * 0.044715) * (x * x)))


def _ada_kernel(c_ref, w_ref, b_ref, o_ref):
    c = c_ref[...]
    ca = (c * jax.nn.sigmoid(c)).astype(BF16)
    o_ref[...] = jnp.dot(ca, w_ref[...].astype(BF16), preferred_element_type=F32) + b_ref[...]


def _ada(c, w, b, tn=1024):
    bsz, d = c.shape
    n = w.shape[1]
    return pl.pallas_call(
        _ada_kernel,
        grid=(n // tn,),
        in_specs=[pl.BlockSpec((bsz, d), lambda j: (0, 0)),
                  pl.BlockSpec((d, tn), lambda j: (0, j)),
                  pl.BlockSpec((1, tn), lambda j: (0, j))],
        out_specs=pl.BlockSpec((bsz, tn), lambda j: (0, j)),
        out_shape=jax.ShapeDtypeStruct((bsz, n), F32),
        compiler_params=_cparams(("arbitrary",)),
        name="ada",
    )(c, w, b.reshape(1, n))


INPROJ_PIECES = 2


def _inproj_kernel(x_ref, sc_ref, sh_ref, g_ref, w_ref, *rest, d_ssm, n_cast):
    cast_in = rest[:n_cast]
    zs_ref, zuv_ref = rest[n_cast:n_cast + 2]
    cast_out = rest[n_cast + 2:]
    gs = g_ref[...] * (1.0 + sc_ref[0])
    tm = x_ref.shape[1]
    piece = tm // INPROJ_PIECES
    for r0 in range(0, tm, piece):
        rows = slice(r0, r0 + piece)
        h = _rms(x_ref[0, rows], gs) + sh_ref[0]
        z = jnp.dot(h.astype(BF16), w_ref[...], preferred_element_type=F32)
        for j in range(d_ssm // LANES):
            zs_ref[0, j, rows] = z[:, j * LANES:(j + 1) * LANES]
        zuv_ref[0, rows] = z[:, d_ssm:].astype(zuv_ref.dtype)
        if r0 == 0:
            for src, dst in zip(cast_in, cast_out):
                dst[...] = src[...].astype(dst.dtype)


def _inproj(x, sc, sh, g, w_bf16, d_ssm, later_weights, tm=512):
    bsz, s, d = x.shape
    d_in = w_bf16.shape[1]
    nslab = d_ssm // LANES
    nt = s // tm
    steps = bsz * nt
    wspecs = [pl.BlockSpec((w.shape[0] // steps, w.shape[1]), lambda b, i: (b * nt + i, 0))
              for w in later_weights]
    return pl.pallas_call(
        functools.partial(_inproj_kernel, d_ssm=d_ssm, n_cast=len(later_weights)),
        grid=(bsz, nt),
        in_specs=[pl.BlockSpec((1, tm, d), lambda b, i: (b, i, 0)),
                  pl.BlockSpec((1, 1, d), lambda b, i: (b, 0, 0)),
                  pl.BlockSpec((1, 1, d), lambda b, i: (b, 0, 0)),
                  _const_spec((1, d)),
                  _const_spec((d, d_in))] + wspecs,
        out_specs=[pl.BlockSpec((1, nslab, tm, LANES), lambda b, i: (b, 0, i, 0)),
                   pl.BlockSpec((1, tm, d_in - d_ssm), lambda b, i: (b, i, 0))] + wspecs,
        out_shape=[jax.ShapeDtypeStruct((bsz, nslab, s, LANES), F32),
                   jax.ShapeDtypeStruct((bsz, s, d_in - d_ssm), BF16)]
                  + [jax.ShapeDtypeStruct(w.shape, BF16) for w in later_weights],
        compiler_params=_cparams(("arbitrary", "arbitrary")),
        name="inproj",
    )(x, sc, sh, g.reshape(1, d), w_bf16, *later_weights)


def _tables_kernel(lp_ref, lt_ref, bt_ref, ct_ref, d_ref, wglu_ref, erep_ref, etile_ref, w_ref,
                   m_ref, p_ref, q_ref, wg_ref, wb_ref):
    wb_ref[...] = w_ref[...].astype(wb_ref.dtype)
    hp = lax.Precision.HIGHEST
    gb = lp_ref.shape[0]
    L = lp_ref.shape[2] - 1
    n = lp_ref.shape[3]
    c = bt_ref.shape[2]
    w = L * c
    lane = lax.broadcasted_iota(jnp.int32, (c, w), 1)
    row = lax.broadcasted_iota(jnp.int32, (c, w), 0)

    def spread(x, e_ref):
        hi = x.astype(BF16)
        r = x - hi.astype(F32)
        mid = r.astype(BF16)
        lo = (r - mid.astype(F32)).astype(BF16)
        e = e_ref[...]
        return (jnp.dot(hi, e, preferred_element_type=F32) + jnp.dot(mid, e, preferred_element_type=F32)
                + jnp.dot(lo, e, preferred_element_type=F32))

    for g in range(gb):
        lq = [jnp.concatenate([jnp.broadcast_to(lp_ref[g, k, L - 1 - s:L - s, :], (c, n)) for s in range(L)],
                              axis=0) for k in range(2)]
        bt = [jnp.concatenate([bt_ref[g, k]] * L, axis=0) for k in range(2)]
        p_ref[g, :, :n] = (lq[0] * bt[0] - lq[1] * bt[1]).astype(p_ref.dtype)
        p_ref[g, :, n:] = (lq[0] * bt[1] + lq[1] * bt[0]).astype(p_ref.dtype)
        cx = [spread(ct_ref[g, k], etile_ref) for k in range(2)]
        lx = [spread(lt_ref[g, k], erep_ref) for k in range(2)]
        v0_re = cx[0] * lx[0] - cx[1] * lx[1]
        v0_im = cx[0] * lx[1] + cx[1] * lx[0]
        lam_re = lt_ref[g, 0][:, 1:2]
        lam_im = lt_ref[g, 1][:, 1:2]
        v1_re = v0_re * lam_re - v0_im * lam_im
        v1_im = v0_re * lam_im + v0_im * lam_re
        q_ref[g, :n, :] = v1_re.astype(q_ref.dtype)
        q_ref[g, n:, :] = (-v1_im).astype(q_ref.dtype)
        kc = (jnp.dot(bt_ref[g, 0], v0_re, precision=hp, preferred_element_type=F32)
              - jnp.dot(bt_ref[g, 1], v0_im, precision=hp, preferred_element_type=F32))
        kc = kc + jnp.where(lane == row, d_ref[g], 0.0)
        for s in range(L):
            blk = kc if s == 0 else jnp.where(lane >= s * c, pltpu.roll(kc, s * c, 1), 0.0)
            m_ref[g, s * c:(s + 1) * c, :] = blk.astype(m_ref.dtype)
        pos = lax.rem(pl.program_id(0) * gb + g, w // c)
        wrow = jnp.where(lane // c == pos, spread(wglu_ref[g], etile_ref), 0.0)
        wg_ref[0, g * c:(g + 1) * c, :] = wrow.astype(wg_ref.dtype)


def _tables(lp, bbt, ct, d, w_glu, w_in, gb=8):
    g, _, lp1, n = lp.shape
    wrows = w_in.shape[0] // (g // gb)
    L = lp1 - 1
    c = bbt.shape[2]
    w = L * c
    lt = lp[:, :, :L].transpose(0, 1, 3, 2)
    col = np.arange(w)
    e_rep = jnp.asarray(col[None, :] // c == np.arange(L)[:, None], dtype=BF16)
    e_tile = jnp.asarray(col[None, :] % c == np.arange(c)[:, None], dtype=BF16)
    blk = lambda *s: pl.BlockSpec((gb,) + s, lambda i: (i,) + (0,) * len(s))
    per_slab = w // (gb * c)
    return pl.pallas_call(
        _tables_kernel,
        grid=(g // gb,),
        in_specs=[blk(2, lp1, n), blk(2, n, L), blk(2, c, n), blk(2, n, c), blk(c, 1), blk(c, c),
                  _const_spec((L, w)), _const_spec((c, w)),
                  pl.BlockSpec((wrows, w_in.shape[1]), lambda i: (i, 0))],
        out_specs=[blk(w, w), blk(w, 2 * n), blk(2 * n, w),
                   pl.BlockSpec((1, gb * c, w), lambda i: (i // per_slab, i % per_slab, 0)),
                   pl.BlockSpec((wrows, w_in.shape[1]), lambda i: (i, 0))],
        out_shape=[jax.ShapeDtypeStruct((g, w, w), BF16), jax.ShapeDtypeStruct((g, w, 2 * n), BF16),
                   jax.ShapeDtypeStruct((g, 2 * n, w), BF16),
                   jax.ShapeDtypeStruct((g * c // w, w, w), BF16),
                   jax.ShapeDtypeStruct(w_in.shape, BF16)],
        compiler_params=_cparams(("arbitrary",)),
        name="tables",
    )(lp, lt, bbt, ct, d, w_glu, e_rep, e_tile, w_in)


def _ssm_tables(log_dt, a_re, a_im, b_re, b_im, c_re, c_im, d, w_glu, w_in):
    L = SSM_L
    G, N, C = b_re.shape
    dt = jnp.exp(log_dt)[:, None]
    ar_dt = a_re * dt
    ai_dt = a_im * dt
    mag = jnp.exp(ar_dt)
    lb_re = mag * jnp.cos(ai_dt)
    lb_im = mag * jnp.sin(ai_dt)
    den = a_re * a_re + a_im * a_im
    nr = lb_re - 1.0
    ni = lb_im
    f_re = ((nr * a_re + ni * a_im) / den)[:, :, None]
    f_im = ((ni * a_re - nr * a_im) / den)[:, :, None]
    bb_re = f_re * b_re - f_im * b_im
    bb_im = f_re * b_im + f_im * b_re
    tau = jnp.arange(L + 1, dtype=F32)[None, :, None]
    pmag = jnp.exp(ar_dt[:, None, :] * tau)
    lp_re = pmag * jnp.cos(ai_dt[:, None, :] * tau)
    lp_im = pmag * jnp.sin(ai_dt[:, None, :] * tau)
    m, p, q, wg, w_in_bf16 = _tables(jnp.stack([lp_re, lp_im], axis=1),
                                     jnp.stack([bb_re, bb_im], axis=1).transpose(0, 1, 3, 2),
                                     jnp.stack([c_re, c_im], axis=1).transpose(0, 1, 3, 2),
                                     d.reshape(G, C, 1), w_glu, w_in)
    l_re = lp_re[:, L]
    l_im = lp_im[:, L]
    a1 = jnp.concatenate([l_re, l_re], axis=-1)
    a2u = jnp.concatenate([-l_im, l_im], axis=-1)
    a2v = jnp.concatenate([l_im, -l_im], axis=-1)
    return (m, p, q, a1, a2u, a2v, wg), w_in_bf16


def _block_swap_perm():
    idx = np.arange(8 * LANES)
    a, b, c = idx // LANES, (idx // SSM_L) % 8, idx % SSM_L
    return jnp.asarray(idx[None, :] == (b * LANES + a * SSM_L + c)[:, None], dtype=BF16)


def _ssm_kernel(zs_ref, perm_ref, m_ref, p_ref, q_ref, a1_ref, a2u_ref, a2v_ref, wg_ref, bg_ref, go_ref,
                o_ref, zf_ref, xu_ref, xv_ref, hp_ref, ys_ref, hu_ref, hv_ref, *, n, pitch, groups):
    L = SSM_L
    d = groups * L
    nslab = d // LANES

    @pl.when(pl.program_id(1) == 0)
    def _():
        hu_ref[...] = jnp.zeros_like(hu_ref)
        hv_ref[...] = jnp.zeros_like(hv_ref)

    def swap_blocks(pieces):
        lhs = jnp.concatenate([jnp.concatenate(ps, axis=1) for ps in pieces], axis=0)
        return jnp.dot(lhs, perm_ref[...], preferred_element_type=F32).astype(BF16)

    for hh in range(2):
        out = swap_blocks([[zs_ref[0, j, pl.ds(8 * hh + tp, n, stride=L), :].astype(BF16)
                            for tp in range(8)] for j in range(nslab)])
        for j in range(nslab):
            for qq in range(8):
                zf_ref[8 * j + qq, :, hh * LANES:(hh + 1) * LANES] = (
                    out[j * n:(j + 1) * n, qq * LANES:(qq + 1) * LANES])

    gx = 8

    def x_body(i, carry):
        for gi in range(gx):
            g = i * gx + gi
            xo = jnp.dot(zf_ref[g], p_ref[g], preferred_element_type=F32)
            r = pl.ds(pl.multiple_of(g * pitch, 8), n)
            xu_ref[r, :] = xo
            xv_ref[r, :] = pltpu.roll(xo, LANES // 2, 1)
        return carry

    lax.fori_loop(0, groups // gx, x_body, 0)

    a1 = a1_ref[...]
    a2u = a2u_ref[...]
    a2v = a2v_ref[...]
    ks = 8

    def scan_body(i, carry):
        hu, hv = carry
        for s in range(ks):
            kk = i * ks + s
            u = xu_ref[pl.ds(kk, groups, stride=pitch), :]
            v = xv_ref[pl.ds(kk, groups, stride=pitch), :]
            hp_ref[pl.ds(kk, groups, stride=pitch), :] = hu
            hu, hv = a1 * hu + a2u * hv + u, a1 * hv + a2v * hu + v
        return hu, hv

    hu, hv = lax.fori_loop(0, n // ks, scan_body, (hu_ref[...], hv_ref[...]))
    hu_ref[...] = hu
    hv_ref[...] = hv

    gy = 8

    def y_body(i, carry):
        for gi in range(gy):
            g = i * gy + gi
            hprev = hp_ref[pl.ds(pl.multiple_of(g * pitch, 8), n), :].astype(BF16)
            y = (jnp.dot(zf_ref[g], m_ref[g], preferred_element_type=F32)
                 + jnp.dot(hprev, q_ref[g], preferred_element_type=F32))
            zf_ref[g] = _gelu_tanh(y).astype(BF16)
        return carry

    lax.fori_loop(0, groups // gy, y_body, 0)

    for hh in range(2):
        out = swap_blocks([[zf_ref[8 * j + qq, :, hh * LANES:(hh + 1) * LANES] for qq in range(8)]
                           for j in range(nslab)])
        for j in range(nslab):
            for tp in range(8):
                ys_ref[:, (8 * hh + tp) * d + j * LANES:(8 * hh + tp) * d + (j + 1) * LANES] = (
                    out[j * n:(j + 1) * n, tp * LANES:(tp + 1) * LANES])

    w2 = 2 * LANES
    for t in range(L):
        yb = ys_ref[:, t * d:(t + 1) * d]
        gate = jnp.concatenate(
            [jnp.dot(yb[:, i * w2:(i + 1) * w2], wg_ref[i], preferred_element_type=F32)
             for i in range(d // w2)], axis=1) + bg_ref[...]
        y = _rms(yb.astype(F32) * jax.nn.sigmoid(gate), go_ref[...])
        for j in range(nslab):
            o_ref[0, j, pl.ds(t, n, stride=L), :] = y[:, j * LANES:(j + 1) * LANES]


def _ssm(zs, tables, b_glu, g_out, sb=1024):
    m, p, q, a1, a2u, a2v, wg = tables
    bsz, nslab, s, _ = zs.shape
    d_ssm = nslab * LANES
    groups = m.shape[0]
    L = SSM_L
    n = sb // L
    pitch = n + 8
    perm = _block_swap_perm()
    kern = functools.partial(_ssm_kernel, n=n, pitch=pitch, groups=groups)
    slabs = pl.BlockSpec((1, nslab, sb, LANES), lambda b, i: (b, 0, i, 0))
    return pl.pallas_call(
        kern,
        grid=(bsz, s // sb),
        in_specs=[slabs,
                  _const_spec(perm.shape),
                  _const_spec(m.shape), _const_spec(p.shape), _const_spec(q.shape),
                  _const_spec(a1.shape), _const_spec(a2u.shape), _const_spec(a2v.shape),
                  _const_spec(wg.shape), _const_spec((1, d_ssm)), _const_spec((1, d_ssm))],
        out_specs=slabs,
        out_shape=jax.ShapeDtypeStruct((bsz, nslab, s, LANES), F32),
        scratch_shapes=[pltpu.VMEM((groups, n, 2 * LANES), BF16),
                        pltpu.VMEM((groups * pitch, LANES), F32),
                        pltpu.VMEM((groups * pitch, LANES), F32),
                        pltpu.VMEM((groups * pitch, LANES), F32),
                        pltpu.VMEM((n, L * d_ssm), BF16),
                        pltpu.VMEM((groups, LANES), F32),
                        pltpu.VMEM((groups, LANES), F32)],
        compiler_params=_cparams(("arbitrary", "arbitrary")),
        name="ssm",
    )(zs, perm, m, p, q, a1, a2u, a2v, wg, b_glu.reshape(1, d_ssm), g_out.reshape(1, d_ssm))


def _sgu_kernel(zuv_ref, lng_ref, lnb_ref, w_ref, bs_ref, go_ref, o_ref, *, heads, hd):
    d = heads * hd
    ts = zuv_ref.shape[1]
    nc = ts // SGU_CHUNK
    u = _gelu_tanh(zuv_ref[0, :, :d].astype(F32))
    v = _gelu_tanh(zuv_ref[0, :, d:].astype(F32))
    mu = jnp.mean(v, axis=-1, keepdims=True)
    vc = v - mu
    v = vc * lax.rsqrt(jnp.mean(vc * vc, axis=-1, keepdims=True) + EPS) * lng_ref[...] + lnb_ref[...]
    vb = v.astype(BF16)
    cols = []
    for h in range(heads):
        vh = jnp.concatenate([vb[c * SGU_CHUNK:(c + 1) * SGU_CHUNK, h * hd:(h + 1) * hd]
                              for c in range(nc)], axis=1)
        mixed = jnp.dot(w_ref[h], vh, preferred_element_type=F32)
        mixed = jnp.concatenate([mixed[:, c * hd:(c + 1) * hd] for c in range(nc)], axis=0)
        cols.append(mixed)
    mixed = jnp.concatenate(cols, axis=1)
    bias = jnp.concatenate([bs_ref[...]] * nc, axis=0)
    y = u * (mixed + bias)
    o_ref[0] = _rms(y, go_ref[...]).astype(o_ref.dtype)


def _sgu(zuv, ln_g, ln_b, w_masked_bf16, b_full, g_out, heads, ts=512):
    bsz, s, d2 = zuv.shape
    d = d2 // 2
    hd = d // heads
    return pl.pallas_call(
        functools.partial(_sgu_kernel, heads=heads, hd=hd),
        grid=(bsz, s // ts),
        in_specs=[pl.BlockSpec((1, ts, d2), lambda b, i: (b, i, 0)),
                  _const_spec((1, d)), _const_spec((1, d)),
                  _const_spec(w_masked_bf16.shape), _const_spec(b_full.shape), _const_spec((1, d))],
        out_specs=pl.BlockSpec((1, ts, d), lambda b, i: (b, i, 0)),
        out_shape=jax.ShapeDtypeStruct((bsz, s, d), BF16),
        compiler_params=_cparams(("arbitrary", "arbitrary")),
        name="sgu",
    )(zuv, ln_g.reshape(1, d), ln_b.reshape(1, d), w_masked_bf16, b_full, g_out.reshape(1, d))


OUTPROJ_PIECES = 4


def _outproj_kernel(ya_ref, yb_ref, x_ref, gt_ref, g_ref, wa_ref, wb_ref, sc_ref, sh_ref, gn_ref,
                    o_ref, h_ref):
    piece = x_ref.shape[1] // OUTPROJ_PIECES
    gg = g_ref[...] * gt_ref[0]
    gs = gn_ref[...] * (1.0 + sc_ref[0])
    for r0 in range(0, x_ref.shape[1], piece):
        rows = slice(r0, r0 + piece)
        ya = jnp.concatenate([ya_ref[0, j, rows] for j in range(ya_ref.shape[1])], axis=1).astype(BF16)
        y = (jnp.dot(ya, wa_ref[...], preferred_element_type=F32)
             + jnp.dot(yb_ref[0, rows], wb_ref[...], preferred_element_type=F32))
        x1 = x_ref[0, rows] + _rms(y, gg)
        o_ref[0, rows] = x1
        h_ref[0, rows] = (_rms(x1, gs) + sh_ref[0]).astype(h_ref.dtype)


def _outproj(ya, yb, x, gt, g, w_bf16, sc_next, sh_next, g_next, tm=512):
    bsz, s, d = x.shape
    nslab = ya.shape[1]
    db = yb.shape[-1]
    assert nslab * LANES == db and w_bf16.shape[0] == 2 * db
    row = lambda w: pl.BlockSpec((1, tm, w), lambda b, i: (b, i, 0))
    per_batch = pl.BlockSpec((1, 1, d), lambda b, i: (b, 0, 0))
    half = lambda k: pl.BlockSpec((db, d), lambda b, i: (k, 0), pipeline_mode=pl.Buffered(1))
    return pl.pallas_call(
        _outproj_kernel,
        grid=(bsz, s // tm),
        in_specs=[pl.BlockSpec((1, nslab, tm, LANES), lambda b, i: (b, 0, i, 0)), row(db), row(d),
                  per_batch, _const_spec((1, d)), half(0), half(1),
                  per_batch, per_batch, _const_spec((1, d))],
        out_specs=[row(d), row(d)],
        out_shape=[jax.ShapeDtypeStruct((bsz, s, d), F32), jax.ShapeDtypeStruct((bsz, s, d), BF16)],
        compiler_params=_cparams(("arbitrary", "arbitrary")),
        name="outproj",
    )(ya, yb, x, gt, g.reshape(1, d), w_bf16, w_bf16, sc_next, sh_next, g_next.reshape(1, d))


FFN_HALO = 16
def _ffn_up_kernel(hin_ref, hh_ref, wa_ref, wb_ref, cwa_ref, cwb_ref, cba_ref, cbb_ref, o_ref, h_ref):
    i = pl.program_id(1)
    j = pl.program_id(2)

    @pl.when(j == 0)
    def _():
        halo = hh_ref[0]
        h_ref[:FFN_HALO, :] = jnp.where(i > 0, halo, jnp.zeros_like(halo))
        h_ref[FFN_HALO:, :] = hin_ref[0]

    h = h_ref[...]

    def branch(w_ref, cw_ref, cb_ref):
        up = jnp.dot(h, w_ref[...], preferred_element_type=F32)
        cw = cw_ref[...]
        r = (cw[0:1] * pltpu.roll(up, 2, 0) + cw[1:2] * pltpu.roll(up, 1, 0) + cw[2:3] * up)
        return r[FFN_HALO:] + cb_ref[...]

    a = branch(wa_ref, cwa_ref, cba_ref)
    b = branch(wb_ref, cwb_ref, cbb_ref)
    o_ref[0] = (a * jax.nn.sigmoid(a) * b).astype(o_ref.dtype)


def _ffn_up(h, w_up_bf16, conv_w, conv_b, tm=1024, tf=512):
    bsz, s, d = h.shape
    f = w_up_bf16.shape[1] // 2
    nf = f // tf
    hb = tm // FFN_HALO
    cb = conv_b.reshape(1, 2 * f)
    return pl.pallas_call(
        _ffn_up_kernel,
        grid=(bsz, s // tm, nf),
        in_specs=[pl.BlockSpec((1, tm, d), lambda b, i, j: (b, i, 0)),
                  pl.BlockSpec((1, FFN_HALO, d), lambda b, i, j: (b, jnp.maximum(i * hb - 1, 0), 0)),
                  pl.BlockSpec((d, tf), lambda b, i, j: (0, j)),
                  pl.BlockSpec((d, tf), lambda b, i, j: (0, nf + j)),
                  pl.BlockSpec((3, tf), lambda b, i, j: (0, j)),
                  pl.BlockSpec((3, tf), lambda b, i, j: (0, nf + j)),
                  pl.BlockSpec((1, tf), lambda b, i, j: (0, j)),
                  pl.BlockSpec((1, tf), lambda b, i, j: (0, nf + j))],
        out_specs=pl.BlockSpec((1, tm, tf), lambda b, i, j: (b, i, j)),
        out_shape=jax.ShapeDtypeStruct((bsz, s, f), BF16),
        scratch_shapes=[pltpu.VMEM((FFN_HALO + tm, d), BF16)],
        compiler_params=_cparams(("arbitrary", "arbitrary", "arbitrary")),
        name="ffn_up",
    )(h, h, w_up_bf16, w_up_bf16, conv_w, conv_w, cb, cb)


FFN_DOWN_PIECES = 2


def _ffn_down_kernel(act_ref, x_ref, gt_ref, g_ref, wd_ref, o_ref):
    gg = g_ref[...] * gt_ref[0]
    piece = x_ref.shape[1] // FFN_DOWN_PIECES
    for r0 in range(0, x_ref.shape[1], piece):
        rows = slice(r0, r0 + piece)
        f = jnp.dot(act_ref[0, rows], wd_ref[...], preferred_element_type=F32)
        o_ref[0, rows] = x_ref[0, rows] + _rms(f, gg)


def _ffn_down(act, x, gt, g_post, w_down_bf16, tm=512):
    bsz, s, d = x.shape
    f = act.shape[-1]
    return pl.pallas_call(
        _ffn_down_kernel,
        grid=(bsz, s // tm),
        in_specs=[pl.BlockSpec((1, tm, f), lambda b, i: (b, i, 0)),
                  pl.BlockSpec((1, tm, d), lambda b, i: (b, i, 0)),
                  pl.BlockSpec((1, 1, d), lambda b, i: (b, 0, 0)),
                  _const_spec((1, d)), _const_spec((f, d))],
        out_specs=pl.BlockSpec((1, tm, d), lambda b, i: (b, i, 0)),
        out_shape=jax.ShapeDtypeStruct((bsz, s, d), F32),
        compiler_params=_cparams(("arbitrary", "arbitrary")),
        name="ffn_down",
    )(act, x, gt, g_post.reshape(1, d), w_down_bf16)


def kernel(x, c, w_ada, b_ada, g_pre_mix, g_post_mix, w_in, ssm_log_dt, ssm_a_re, ssm_a_im, ssm_b_re, ssm_b_im, ssm_c_re, ssm_c_im, ssm_d, ssm_w_glu, ssm_b_glu, sgu_ln_g, sgu_ln_b, sgu_w, sgu_b, g_out_ssm, g_out_sgu, w_out, g_pre_ffn, g_post_ffn, w_up, conv_w, conv_b, w_down):
    depth = w_ada.shape[0]
    d = x.shape[-1]
    d_ssm = ssm_d.shape[-1]
    heads = sgu_w.shape[1]
    hd = (w_out.shape[1] - d_ssm) // heads
    for l in range(depth):
        mod = _ada(c, w_ada[l], b_ada[l])
        sh1, sc1, gt1, sh2, sc2, gt2 = [mod[:, None, k * d:(k + 1) * d] for k in range(6)]

        tables, wi = _ssm_tables(ssm_log_dt[l], ssm_a_re[l], ssm_a_im[l], ssm_b_re[l], ssm_b_im[l],
                                 ssm_c_re[l], ssm_c_im[l], ssm_d[l], ssm_w_glu[l], w_in[l])
        zs, zuv, wo, wu, wd = _inproj(x, sc1, sh1, g_pre_mix[l], wi, d_ssm,
                                      (w_out[l], w_up[l], w_down[l]))
        y_ssm = _ssm(zs, tables, ssm_b_glu[l], g_out_ssm[l])
        w_s = jnp.tril(sgu_w[l]).astype(BF16)
        b_full = jnp.repeat(sgu_b[l].T, hd, axis=1)
        y_sgu = _sgu(zuv, sgu_ln_g[l], sgu_ln_b[l], w_s, b_full, g_out_sgu[l], heads)
        x, h = _outproj(y_ssm, y_sgu, x, gt1, g_post_mix[l], wo, sc2, sh2, g_pre_ffn[l])

        act = _ffn_up(h, wu, conv_w[l], conv_b[l])
        x = _ffn_down(act, x, gt2, g_post_ffn[l], wd)
    return x
```

```python
import functools
import math

import jax
import jax.numpy as jnp
import numpy as np
from jax import lax
from jax.experimental import pallas as pl
from jax.experimental.pallas import tpu as pltpu

F32 = jnp.float32
BF16 = jnp.bfloat16
EPS = 1e-6

LANES = 128
SSM_L = 16
SGU_CHUNK = 128
VMEM_LIMIT = 60 * 1024 * 1024


def _cparams(sem):
    return pltpu.CompilerParams(dimension_semantics=sem, vmem_limit_bytes=VMEM_LIMIT)


def _const_spec(shape):
    nd = len(shape)
    return pl.BlockSpec(shape, lambda *_: (0,) * nd, pipeline_mode=pl.Buffered(1))


def _rms(xf, g):
    return xf * lax.rsqrt(jnp.mean(xf * xf, axis=-1, keepdims=True) + EPS) * g


def _gelu_tanh(x):
    k = math.sqrt(2.0 / math.pi)
    hx = 0.5 * x
    return hx + hx * jnp.tanh(x * (k + (k * 0.044715) * (x * x)))


def _ada_kernel(c_ref, w_ref, b_ref, o_ref):
    c = c_ref[...]
    ca = (c * jax.nn.sigmoid(c)).astype(BF16)
    o_ref[...] = jnp.dot(ca, w_ref[...].astype(BF16), preferred_element_type=F32) + b_ref[...]


def _ada(c, w, b, tn=1024):
    bsz, d = c.shape
    n = w.shape[1]
    return pl.pallas_call(
        _ada_kernel,
        grid=(n // tn,),
        in_specs=[pl.BlockSpec((bsz, d), lambda j: (0, 0)),
                  pl.BlockSpec((d, tn), lambda j: (0, j)),
                  pl.BlockSpec((1, tn), lambda j: (0, j))],
        out_specs=pl.BlockSpec((bsz, tn), lambda j: (0, j)),
        out_shape=jax.ShapeDtypeStruct((bsz, n), F32),
        compiler_params=_cparams(("arbitrary",)),
        name="ada",
    )(c, w, b.reshape(1, n))


INPROJ_PIECES = 2


def _inproj_kernel(x_ref, sc_ref, sh_ref, g_ref, w_ref, *rest, d_ssm, n_cast):
    cast_in = rest[:n_cast]
    zs_ref, zuv_ref = rest[n_cast:n_cast + 2]
    cast_out = rest[n_cast + 2:]
    gs = g_ref[...] * (1.0 + sc_ref[0])
    tm = x_ref.shape[1]
    piece = tm // INPROJ_PIECES
    for r0 in range(0, tm, piece):
        rows = slice(r0, r0 + piece)
        h = _rms(x_ref[0, rows], gs) + sh_ref[0]
        z = jnp.dot(h.astype(BF16), w_ref[...], preferred_element_type=F32)
        for j in range(d_ssm // LANES):
            zs_ref[0, j, rows] = z[:, j * LANES:(j + 1) * LANES]
        zuv_ref[0, rows] = z[:, d_ssm:].astype(zuv_ref.dtype)
        if r0 == 0:
            for src, dst in zip(cast_in, cast_out):
                dst[...] = src[...].astype(dst.dtype)


def _inproj(x, sc, sh, g, w_bf16, d_ssm, later_weights, tm=512):
    bsz, s, d = x.shape
    d_in = w_bf16.shape[1]
    nslab = d_ssm // LANES
    nt = s // tm
    steps = bsz * nt
    wspecs = [pl.BlockSpec((w.shape[0] // steps, w.shape[1]), lambda b, i: (b * nt + i, 0))
              for w in later_weights]
    return pl.pallas_call(
        functools.partial(_inproj_kernel, d_ssm=d_ssm, n_cast=len(later_weights)),
        grid=(bsz, nt),
        in_specs=[pl.BlockSpec((1, tm, d), lambda b, i: (b, i, 0)),
                  pl.BlockSpec((1, 1, d), lambda b, i: (b, 0, 0)),
                  pl.BlockSpec((1, 1, d), lambda b, i: (b, 0, 0)),
                  _const_spec((1, d)),
                  _const_spec((d, d_in))] + wspecs,
        out_specs=[pl.BlockSpec((1, nslab, tm, LANES), lambda b, i: (b, 0, i, 0)),
                   pl.BlockSpec((1, tm, d_in - d_ssm), lambda b, i: (b, i, 0))] + wspecs,
        out_shape=[jax.ShapeDtypeStruct((bsz, nslab, s, LANES), F32),
                   jax.ShapeDtypeStruct((bsz, s, d_in - d_ssm), BF16)]
                  + [jax.ShapeDtypeStruct(w.shape, BF16) for w in later_weights],
        compiler_params=_cparams(("arbitrary", "arbitrary")),
        name="inproj",
    )(x, sc, sh, g.reshape(1, d), w_bf16, *later_weights)


def _tables_kernel(lp_ref, lt_ref, bt_ref, ct_ref, d_ref, wglu_ref, erep_ref, etile_ref, w_ref,
                   m_ref, p_ref, q_ref, wg_ref, wb_ref):
    wb_ref[...] = w_ref[...].astype(wb_ref.dtype)
    hp = lax.Precision.HIGHEST
    gb = lp_ref.shape[0]
    L = lp_ref.shape[2] - 1
    n = lp_ref.shape[3]
    c = bt_ref.shape[2]
    w = L * c
    lane = lax.broadcasted_iota(jnp.int32, (c, w), 1)
    row = lax.broadcasted_iota(jnp.int32, (c, w), 0)

    def spread(x, e_ref):
        hi = x.astype(BF16)
        r = x - hi.astype(F32)
        mid = r.astype(BF16)
        lo = (r - mid.astype(F32)).astype(BF16)
        e = e_ref[...]
        return (jnp.dot(hi, e, preferred_element_type=F32) + jnp.dot(mid, e, preferred_element_type=F32)
                + jnp.dot(lo, e, preferred_element_type=F32))

    for g in range(gb):
        lq = [jnp.concatenate([jnp.broadcast_to(lp_ref[g, k, L - 1 - s:L - s, :], (c, n)) for s in range(L)],
                              axis=0) for k in range(2)]
        bt = [jnp.concatenate([bt_ref[g, k]] * L, axis=0) for k in range(2)]
        p_ref[g, :, :n] = (lq[0] * bt[0] - lq[1] * bt[1]).astype(p_ref.dtype)
        p_ref[g, :, n:] = (lq[0] * bt[1] + lq[1] * bt[0]).astype(p_ref.dtype)
        cx = [spread(ct_ref[g, k], etile_ref) for k in range(2)]
        lx = [spread(lt_ref[g, k], erep_ref) for k in range(2)]
        v0_re = cx[0] * lx[0] - cx[1] * lx[1]
        v0_im = cx[0] * lx[1] + cx[1] * lx[0]
        lam_re = lt_ref[g, 0][:, 1:2]
        lam_im = lt_ref[g, 1][:, 1:2]
        v1_re = v0_re * lam_re - v0_im * lam_im
        v1_im = v0_re * lam_im + v0_im * lam_re
        q_ref[g, :n, :] = v1_re.astype(q_ref.dtype)
        q_ref[g, n:, :] = (-v1_im).astype(q_ref.dtype)
        kc = (jnp.dot(bt_ref[g, 0], v0_re, precision=hp, preferred_element_type=F32)
              - jnp.dot(bt_ref[g, 1], v0_im, precision=hp, preferred_element_type=F32))
        kc = kc + jnp.where(lane == row, d_ref[g], 0.0)
        for s in range(L):
            blk = kc if s == 0 else jnp.where(lane >= s * c, pltpu.roll(kc, s * c, 1), 0.0)
            m_ref[g, s * c:(s + 1) * c, :] = blk.astype(m_ref.dtype)
        pos = lax.rem(pl.program_id(0) * gb + g, w // c)
        wrow = jnp.where(lane // c == pos, spread(wglu_ref[g], etile_ref), 0.0)
        wg_ref[0, g * c:(g + 1) * c, :] = wrow.astype(wg_ref.dtype)


def _tables(lp, bbt, ct, d, w_glu, w_in, gb=8):
    g, _, lp1, n = lp.shape
    wrows = w_in.shape[0] // (g // gb)
    L = lp1 - 1
    c = bbt.shape[2]
    w = L * c
    lt = lp[:, :, :L].transpose(0, 1, 3, 2)
    col = np.arange(w)
    e_rep = jnp.asarray(col[None, :] // c == np.arange(L)[:, None], dtype=BF16)
    e_tile = jnp.asarray(col[None, :] % c == np.arange(c)[:, None], dtype=BF16)
    blk = lambda *s: pl.BlockSpec((gb,) + s, lambda i: (i,) + (0,) * len(s))
    per_slab = w // (gb * c)
    return pl.pallas_call(
        _tables_kernel,
        grid=(g // gb,),
        in_specs=[blk(2, lp1, n), blk(2, n, L), blk(2, c, n), blk(2, n, c), blk(c, 1), blk(c, c),
                  _const_spec((L, w)), _const_spec((c, w)),
                  pl.BlockSpec((wrows, w_in.shape[1]), lambda i: (i, 0))],
        out_specs=[blk(w, w), blk(w, 2 * n), blk(2 * n, w),
                   pl.BlockSpec((1, gb * c, w), lambda i: (i // per_slab, i % per_slab, 0)),
                   pl.BlockSpec((wrows, w_in.shape[1]), lambda i: (i, 0))],
        out_shape=[jax.ShapeDtypeStruct((g, w, w), BF16), jax.ShapeDtypeStruct((g, w, 2 * n), BF16),
                   jax.ShapeDtypeStruct((g, 2 * n, w), BF16),
                   jax.ShapeDtypeStruct((g * c // w, w, w), BF16),
                   jax.ShapeDtypeStruct(w_in.shape, BF16)],
        compiler_params=_cparams(("arbitrary",)),
        name="tables",
    )(lp, lt, bbt, ct, d, w_glu, e_rep, e_tile, w_in)


def _ssm_tables(log_dt, a_re, a_im, b_re, b_im, c_re, c_im, d, w_glu, w_in):
    L = SSM_L
    G, N, C = b_re.shape
    dt = jnp.exp(log_dt)[:, None]
    ar_dt = a_re * dt
    ai_dt = a_im * dt
    mag = jnp.exp(ar_dt)
    lb_re = mag * jnp.cos(ai_dt)
    lb_im = mag * jnp.sin(ai_dt)
    den = a_re * a_re + a_im * a_im
    nr = lb_re - 1.0
    ni = lb_im
    f_re = ((nr * a_re + ni * a_im) / den)[:, :, None]
    f_im = ((ni * a_re - nr * a_im) / den)[:, :, None]
    bb_re = f_re * b_re - f_im * b_im
    bb_im = f_re * b_im + f_im * b_re
    tau = jnp.arange(L + 1, dtype=F32)[None, :, None]
    pmag = jnp.exp(ar_dt[:, None, :] * tau)
    lp_re = pmag * jnp.cos(ai_dt[:, None, :] * tau)
    lp_im = pmag * jnp.sin(ai_dt[:, None, :] * tau)
    m, p, q, wg, w_in_bf16 = _tables(jnp.stack([lp_re, lp_im], axis=1),
                                     jnp.stack([bb_re, bb_im], axis=1).transpose(0, 1, 3, 2),
                                     jnp.stack([c_re, c_im], axis=1).transpose(0, 1, 3, 2),
                                     d.reshape(G, C, 1), w_glu, w_in)
    l_re = lp_re[:, L]
    l_im = lp_im[:, L]
    a1 = jnp.concatenate([l_re, l_re], axis=-1)
    a2u = jnp.concatenate([-l_im, l_im], axis=-1)
    a2v = jnp.concatenate([l_im, -l_im], axis=-1)
    return (m, p, q, a1, a2u, a2v, wg), w_in_bf16


def _block_swap_perm():
    idx = np.arange(8 * LANES)
    a, b, c = idx // LANES, (idx // SSM_L) % 8, idx % SSM_L
    return jnp.asarray(idx[None, :] == (b * LANES + a * SSM_L + c)[:, None], dtype=BF16)


def _ssm_kernel(zs_ref, perm_ref, m_ref, p_ref, q_ref, a1_ref, a2u_ref, a2v_ref, wg_ref, bg_ref, go_ref,
                o_ref, zf_ref, xu_ref, xv_ref, hp_ref, ys_ref, hu_ref, hv_ref, *, n, pitch, groups):
    L = SSM_L
    d = groups * L
    nslab = d // LANES

    @pl.when(pl.program_id(1) == 0)
    def _():
        hu_ref[...] = jnp.zeros_like(hu_ref)
        hv_ref[...] = jnp.zeros_like(hv_ref)

    def swap_blocks(pieces):
        lhs = jnp.concatenate([jnp.concatenate(ps, axis=1) for ps in pieces], axis=0)
        return jnp.dot(lhs, perm_ref[...], preferred_element_type=F32).astype(BF16)

    for hh in range(2):
        out = swap_blocks([[zs_ref[0, j, pl.ds(8 * hh + tp, n, stride=L), :].astype(BF16)
                            for tp in range(8)] for j in range(nslab)])
        for j in range(nslab):
            for qq in range(8):
                zf_ref[8 * j + qq, :, hh * LANES:(hh + 1) * LANES] = (
                    out[j * n:(j + 1) * n, qq * LANES:(qq + 1) * LANES])

    gx = 64

    def x_body(i, carry):
        for gi in range(gx):
            g = i * gx + gi
            xo = jnp.dot(zf_ref[g], p_ref[g], preferred_element_type=F32)
            r = pl.ds(pl.multiple_of(g * pitch, 8), n)
            xu_ref[r, :] = xo
            xv_ref[r, :] = pltpu.roll(xo, LANES // 2, 1)
        return carry

    lax.fori_loop(0, groups // gx, x_body, 0)

    a1 = a1_ref[...]
    a2u = a2u_ref[...]
    a2v = a2v_ref[...]
    ks = 8

    def scan_body(i, carry):
        hu, hv = carry
        for s in range(ks):
            kk = i * ks + s
            u = xu_ref[pl.ds(kk, groups, stride=pitch), :]
            v = xv_ref[pl.ds(kk, groups, stride=pitch), :]
            hp_ref[pl.ds(kk, groups, stride=pitch), :] = hu
            hu, hv = a1 * hu + a2u * hv + u, a1 * hv + a2v * hu + v
        return hu, hv

    hu, hv = lax.fori_loop(0, n // ks, scan_body, (hu_ref[...], hv_ref[...]))
    hu_ref[...] = hu
    hv_ref[...] = hv

    gy = 64

    def y_body(i, carry):
        for gi in range(gy):
            g = i * gy + gi
            hprev = hp_ref[pl.ds(pl.multiple_of(g * pitch, 8), n), :].astype(BF16)
            y = (jnp.dot(zf_ref[g], m_ref[g], preferred_element_type=F32)
                 + jnp.dot(hprev, q_ref[g], preferred_element_type=F32))
            zf_ref[g] = _gelu_tanh(y).astype(BF16)
        return carry

    lax.fori_loop(0, groups // gy, y_body, 0)

    for hh in range(2):
        out = swap_blocks([[zf_ref[8 * j + qq, :, hh * LANES:(hh + 1) * LANES] for qq in range(8)]
                           for j in range(nslab)])
        for j in range(nslab):
            for tp in range(8):
                ys_ref[:, (8 * hh + tp) * d + j * LANES:(8 * hh + tp) * d + (j + 1) * LANES] = (
                    out[j * n:(j + 1) * n, tp * LANES:(tp + 1) * LANES])

    w2 = 2 * LANES
    for t in range(L):
        yb = ys_ref[:, t * d:(t + 1) * d]
        gate = jnp.concatenate(
            [jnp.dot(yb[:, i * w2:(i + 1) * w2], wg_ref[i], preferred_element_type=F32)
             for i in range(d // w2)], axis=1) + bg_ref[...]
        y = _rms(yb.astype(F32) * jax.nn.sigmoid(gate), go_ref[...])
        for j in range(nslab):
            o_ref[0, j, pl.ds(t, n, stride=L), :] = y[:, j * LANES:(j + 1) * LANES]


def _ssm(zs, tables, b_glu, g_out, sb=1024):
    m, p, q, a1, a2u, a2v, wg = tables
    bsz, nslab, s, _ = zs.shape
    d_ssm = nslab * LANES
    groups = m.shape[0]
    L = SSM_L
    n = sb // L
    pitch = n + 8
    perm = _block_swap_perm()
    kern = functools.partial(_ssm_kernel, n=n, pitch=pitch, groups=groups)
    slabs = pl.BlockSpec((1, nslab, sb, LANES), lambda b, i: (b, 0, i, 0))
    return pl.pallas_call(
        kern,
        grid=(bsz, s // sb),
        in_specs=[slabs,
                  _const_spec(perm.shape),
                  _const_spec(m.shape), _const_spec(p.shape), _const_spec(q.shape),
                  _const_spec(a1.shape), _const_spec(a2u.shape), _const_spec(a2v.shape),
                  _const_spec(wg.shape), _const_spec((1, d_ssm)), _const_spec((1, d_ssm))],
        out_specs=slabs,
        out_shape=jax.ShapeDtypeStruct((bsz, nslab, s, LANES), F32),
        scratch_shapes=[pltpu.VMEM((groups, n, 2 * LANES), BF16),
                        pltpu.VMEM((groups * pitch, LANES), F32),
                        pltpu.VMEM((groups * pitch, LANES), F32),
                        pltpu.VMEM((groups * pitch, LANES), F32),
                        pltpu.VMEM((n, L * d_ssm), BF16),
                        pltpu.VMEM((groups, LANES), F32),
                        pltpu.VMEM((groups, LANES), F32)],
        compiler_params=_cparams(("arbitrary", "arbitrary")),
        name="ssm",
    )(zs, perm, m, p, q, a1, a2u, a2v, wg, b_glu.reshape(1, d_ssm), g_out.reshape(1, d_ssm))


def _sgu_kernel(zuv_ref, lng_ref, lnb_ref, w_ref, bs_ref, go_ref, o_ref, *, heads, hd):
    d = heads * hd
    ts = zuv_ref.shape[1]
    nc = ts // SGU_CHUNK
    u = _gelu_tanh(zuv_ref[0, :, :d].astype(F32))
    v = _gelu_tanh(zuv_ref[0, :, d:].astype(F32))
    mu = jnp.mean(v, axis=-1, keepdims=True)
    vc = v - mu
    v = vc * lax.rsqrt(jnp.mean(vc * vc, axis=-1, keepdims=True) + EPS) * lng_ref[...] + lnb_ref[...]
    vb = v.astype(BF16)
    cols = []
    for h in range(heads):
        vh = jnp.concatenate([vb[c * SGU_CHUNK:(c + 1) * SGU_CHUNK, h * hd:(h + 1) * hd]
                              for c in range(nc)], axis=1)
        mixed = jnp.dot(w_ref[h], vh, preferred_element_type=F32)
        mixed = jnp.concatenate([mixed[:, c * hd:(c + 1) * hd] for c in range(nc)], axis=0)
        cols.append(mixed)
    mixed = jnp.concatenate(cols, axis=1)
    bias = jnp.concatenate([bs_ref[...]] * nc, axis=0)
    y = u * (mixed + bias)
    o_ref[0] = _rms(y, go_ref[...]).astype(o_ref.dtype)


def _sgu(zuv, ln_g, ln_b, w_masked_bf16, b_full, g_out, heads, ts=512):
    bsz, s, d2 = zuv.shape
    d = d2 // 2
    hd = d // heads
    return pl.pallas_call(
        functools.partial(_sgu_kernel, heads=heads, hd=hd),
        grid=(bsz, s // ts),
        in_specs=[pl.BlockSpec((1, ts, d2), lambda b, i: (b, i, 0)),
                  _const_spec((1, d)), _const_spec((1, d)),
                  _const_spec(w_masked_bf16.shape), _const_spec(b_full.shape), _const_spec((1, d))],
        out_specs=pl.BlockSpec((1, ts, d), lambda b, i: (b, i, 0)),
        out_shape=jax.ShapeDtypeStruct((bsz, s, d), BF16),
        compiler_params=_cparams(("arbitrary", "arbitrary")),
        name="sgu",
    )(zuv, ln_g.reshape(1, d), ln_b.reshape(1, d), w_masked_bf16, b_full, g_out.reshape(1, d))


OUTPROJ_PIECES = 4


def _outproj_kernel(ya_ref, yb_ref, x_ref, gt_ref, g_ref, wa_ref, wb_ref, sc_ref, sh_ref, gn_ref,
                    o_ref, h_ref):
    piece = x_ref.shape[1] // OUTPROJ_PIECES
    gg = g_ref[...] * gt_ref[0]
    gs = gn_ref[...] * (1.0 + sc_ref[0])
    for r0 in range(0, x_ref.shape[1], piece):
        rows = slice(r0, r0 + piece)
        ya = jnp.concatenate([ya_ref[0, j, rows] for j in range(ya_ref.shape[1])], axis=1).astype(BF16)
        y = (jnp.dot(ya, wa_ref[...], preferred_element_type=F32)
             + jnp.dot(yb_ref[0, rows], wb_ref[...], preferred_element_type=F32))
        x1 = x_ref[0, rows] + _rms(y, gg)
        o_ref[0, rows] = x1
        h_ref[0, rows] = (_rms(x1, gs) + sh_ref[0]).astype(h_ref.dtype)


def _outproj(ya, yb, x, gt, g, w_bf16, sc_next, sh_next, g_next, tm=512):
    bsz, s, d = x.shape
    nslab = ya.shape[1]
    db = yb.shape[-1]
    assert nslab * LANES == db and w_bf16.shape[0] == 2 * db
    row = lambda w: pl.BlockSpec((1, tm, w), lambda b, i: (b, i, 0))
    per_batch = pl.BlockSpec((1, 1, d), lambda b, i: (b, 0, 0))
    half = lambda k: pl.BlockSpec((db, d), lambda b, i: (k, 0), pipeline_mode=pl.Buffered(1))
    return pl.pallas_call(
        _outproj_kernel,
        grid=(bsz, s // tm),
        in_specs=[pl.BlockSpec((1, nslab, tm, LANES), lambda b, i: (b, 0, i, 0)), row(db), row(d),
                  per_batch, _const_spec((1, d)), half(0), half(1),
                  per_batch, per_batch, _const_spec((1, d))],
        out_specs=[row(d), row(d)],
        out_shape=[jax.ShapeDtypeStruct((bsz, s, d), F32), jax.ShapeDtypeStruct((bsz, s, d), BF16)],
        compiler_params=_cparams(("arbitrary", "arbitrary")),
        name="outproj",
    )(ya, yb, x, gt, g.reshape(1, d), w_bf16, w_bf16, sc_next, sh_next, g_next.reshape(1, d))


FFN_HALO = 16
def _ffn_up_kernel(hin_ref, hh_ref, wa_ref, wb_ref, cwa_ref, cwb_ref, cba_ref, cbb_ref, o_ref, h_ref):
    i = pl.program_id(1)
    j = pl.program_id(2)

    @pl.when(j == 0)
    def _():
        halo = hh_ref[0]
        h_ref[:FFN_HALO, :] = jnp.where(i > 0, halo, jnp.zeros_like(halo))
        h_ref[FFN_HALO:, :] = hin_ref[0]

    h = h_ref[...]

    def branch(w_ref, cw_ref, cb_ref):
        up = jnp.dot(h, w_ref[...], preferred_element_type=F32)
        cw = cw_ref[...]
        r = (cw[0:1] * pltpu.roll(up, 2, 0) + cw[1:2] * pltpu.roll(up, 1, 0) + cw[2:3] * up)
        return r[FFN_HALO:] + cb_ref[...]

    a = branch(wa_ref, cwa_ref, cba_ref)
    b = branch(wb_ref, cwb_ref, cbb_ref)
    o_ref[0] = (a * jax.nn.sigmoid(a) * b).astype(o_ref.dtype)


def _ffn_up(h, w_up_bf16, conv_w, conv_b, tm=1024, tf=512):
    bsz, s, d = h.shape
    f = w_up_bf16.shape[1] // 2
    nf = f // tf
    hb = tm // FFN_HALO
    cb = conv_b.reshape(1, 2 * f)
    return pl.pallas_call(
        _ffn_up_kernel,
        grid=(bsz, s // tm, nf),
        in_specs=[pl.BlockSpec((1, tm, d), lambda b, i, j: (b, i, 0)),
                  pl.BlockSpec((1, FFN_HALO, d), lambda b, i, j: (b, jnp.maximum(i * hb - 1, 0), 0)),
                  pl.BlockSpec((d, tf), lambda b, i, j: (0, j)),
                  pl.BlockSpec((d, tf), lambda b, i, j: (0, nf + j)),
                  pl.BlockSpec((3, tf), lambda b, i, j: (0, j)),
                  pl.BlockSpec((3, tf), lambda b, i, j: (0, nf + j)),
                  pl.BlockSpec((1, tf), lambda b, i, j: (0, j)),
                  pl.BlockSpec((1, tf), lambda b, i, j: (0, nf + j))],
        out_specs=pl.BlockSpec((1, tm, tf), lambda b, i, j: (b, i, j)),
        out_shape=jax.ShapeDtypeStruct((bsz, s, f), BF16),
        scratch_shapes=[pltpu.VMEM((FFN_HALO + tm, d), BF16)],
        compiler_params=_cparams(("arbitrary", "arbitrary", "arbitrary")),
        name="ffn_up",
    )(h, h, w_up_bf16, w_up_bf16, conv_w, conv_w, cb, cb)


FFN_DOWN_PIECES = 2


def _ffn_down_kernel(act_ref, x_ref, gt_ref, g_ref, wd_ref, o_ref):
    gg = g_ref[...] * gt_ref[0]
    piece = x_ref.shape[1] // FFN_DOWN_PIECES
    for r0 in range(0, x_ref.shape[1], piece):
        rows = slice(r0, r0 + piece)
        f = jnp.dot(act_ref[0, rows], wd_ref[...], preferred_element_type=F32)
        o_ref[0, rows] = x_ref[0, rows] + _rms(f, gg)


def _ffn_down(act, x, gt, g_post, w_down_bf16, tm=512):
    bsz, s, d = x.shape
    f = act.shape[-1]
    return pl.pallas_call(
        _ffn_down_kernel,
        grid=(bsz, s // tm),
        in_specs=[pl.BlockSpec((1, tm, f), lambda b, i: (b, i, 0)),
                  pl.BlockSpec((1, tm, d), lambda b, i: (b, i, 0)),
                  pl.BlockSpec((1, 1, d), lambda b, i: (b, 0, 0)),
                  _const_spec((1, d)), _const_spec((f, d))],
        out_specs=pl.BlockSpec((1, tm, d), lambda b, i: (b, i, 0)),
        out_shape=jax.ShapeDtypeStruct((bsz, s, d), F32),
        compiler_params=_cparams(("arbitrary", "arbitrary")),
        name="ffn_down",
    )(act, x, gt, g_post.reshape(1, d), w_down_bf16)


def kernel(x, c, w_ada, b_ada, g_pre_mix, g_post_mix, w_in, ssm_log_dt, ssm_a_re, ssm_a_im, ssm_b_re, ssm_b_im, ssm_c_re, ssm_c_im, ssm_d, ssm_w_glu, ssm_b_glu, sgu_ln_g, sgu_ln_b, sgu_w, sgu_b, g_out_ssm, g_out_sgu, w_out, g_pre_ffn, g_post_ffn, w_up, conv_w, conv_b, w_down):
    depth = w_ada.shape[0]
    d = x.shape[-1]
    d_ssm = ssm_d.shape[-1]
    heads = sgu_w.shape[1]
    hd = (w_out.shape[1] - d_ssm) // heads
    for l in range(depth):
        mod = _ada(c, w_ada[l], b_ada[l])
        sh1, sc1, gt1, sh2, sc2, gt2 = [mod[:, None, k * d:(k + 1) * d] for k in range(6)]

        tables, wi = _ssm_tables(ssm_log_dt[l], ssm_a_re[l], ssm_a_im[l], ssm_b_re[l], ssm_b_im[l],
                                 ssm_c_re[l], ssm_c_im[l], ssm_d[l], ssm_w_glu[l], w_in[l])
        zs, zuv, wo, wu, wd = _inproj(x, sc1, sh1, g_pre_mix[l], wi, d_ssm,
                                      (w_out[l], w_up[l], w_down[l]))
        y_ssm = _ssm(zs, tables, ssm_b_glu[l], g_out_ssm[l])
        w_s = jnp.tril(sgu_w[l]).astype(BF16)
        b_full = jnp.repeat(sgu_b[l].T, hd, axis=1)
        y_sgu = _sgu(zuv, sgu_ln_g[l], sgu_ln_b[l], w_s, b_full, g_out_sgu[l], heads)
        x, h = _outproj(y_ssm, y_sgu, x, gt1, g_post_mix[l], wo, sc2, sh2, g_pre_ffn[l])

        act = _ffn_up(h, wu, conv_w[l], conv_b[l])
        x = _ffn_down(act, x, gt2, g_post_ffn[l], wd)
    return x
```

```python
import functools
import math

import jax
import jax.numpy as jnp
import numpy as np
from jax import lax
from jax.experimental import pallas as pl
from jax.experimental.pallas import tpu as pltpu

F32 = jnp.float32
BF16 = jnp.bfloat16
EPS = 1e-6

LANES = 128
SSM_L = 16
SGU_CHUNK = 128
VMEM_LIMIT = 60 * 1024 * 1024


def _cparams(sem):
    return pltpu.CompilerParams(dimension_semantics=sem, vmem_limit_bytes=VMEM_LIMIT)


def _const_spec(shape):
    nd = len(shape)
    return pl.BlockSpec(shape, lambda *_: (0,) * nd, pipeline_mode=pl.Buffered(1))


def _rms(xf, g):
    return xf * lax.rsqrt(jnp.mean(xf * xf, axis=-1, keepdims=True) + EPS) * g


def _gelu_tanh(x):
    k = math.sqrt(2.0 / math.pi)
    hx = 0.5 * x
    return hx + hx * jnp.tanh(x * (k + (k * 0.044715) * (x * x)))


def _ada_kernel(c_ref, w_ref, b_ref, o_ref):
    c = c_ref[...]
    ca = (c * jax.nn.sigmoid(c)).astype(BF16)
    o_ref[...] = jnp.dot(ca, w_ref[...].astype(BF16), preferred_element_type=F32) + b_ref[...]


def _ada(c, w, b, tn=1024):
    bsz, d = c.shape
    n = w.shape[1]
    return pl.pallas_call(
        _ada_kernel,
        grid=(n // tn,),
        in_specs=[pl.BlockSpec((bsz, d), lambda j: (0, 0)),
                  pl.BlockSpec((d, tn), lambda j: (0, j)),
                  pl.BlockSpec((1, tn), lambda j: (0, j))],
        out_specs=pl.BlockSpec((bsz, tn), lambda j: (0, j)),
        out_shape=jax.ShapeDtypeStruct((bsz, n), F32),
        compiler_params=_cparams(("arbitrary",)),
        name="ada",
    )(c, w, b.reshape(1, n))


INPROJ_PIECES = 2


def _inproj_kernel(x_ref, sc_ref, sh_ref, g_ref, w_ref, *rest, d_ssm, n_cast):
    cast_in = rest[:n_cast]
    zs_ref, zuv_ref = rest[n_cast:n_cast + 2]
    cast_out = rest[n_cast + 2:]
    gs = g_ref[...] * (1.0 + sc_ref[0])
    tm = x_ref.shape[1]
    piece = tm // INPROJ_PIECES
    for r0 in range(0, tm, piece):
        rows = slice(r0, r0 + piece)
        h = _rms(x_ref[0, rows], gs) + sh_ref[0]
        z = jnp.dot(h.astype(BF16), w_ref[...], preferred_element_type=F32)
        for j in range(d_ssm // LANES):
            zs_ref[0, j, rows] = z[:, j * LANES:(j + 1) * LANES]
        zuv_ref[0, rows] = z[:, d_ssm:].astype(zuv_ref.dtype)
        if r0 == 0:
            for src, dst in zip(cast_in, cast_out):
                dst[...] = src[...].astype(dst.dtype)


def _inproj(x, sc, sh, g, w_bf16, d_ssm, later_weights, tm=512):
    bsz, s, d = x.shape
    d_in = w_bf16.shape[1]
    nslab = d_ssm // LANES
    nt = s // tm
    steps = bsz * nt
    wspecs = [pl.BlockSpec((w.shape[0] // steps, w.shape[1]), lambda b, i: (b * nt + i, 0))
              for w in later_weights]
    return pl.pallas_call(
        functools.partial(_inproj_kernel, d_ssm=d_ssm, n_cast=len(later_weights)),
        grid=(bsz, nt),
        in_specs=[pl.BlockSpec((1, tm, d), lambda b, i: (b, i, 0)),
                  pl.BlockSpec((1, 1, d), lambda b, i: (b, 0, 0)),
                  pl.BlockSpec((1, 1, d), lambda b, i: (b, 0, 0)),
                  _const_spec((1, d)),
                  _const_spec((d, d_in))] + wspecs,
        out_specs=[pl.BlockSpec((1, nslab, tm, LANES), lambda b, i: (b, 0, i, 0)),
                   pl.BlockSpec((1, tm, d_in - d_ssm), lambda b, i: (b, i, 0))] + wspecs,
        out_shape=[jax.ShapeDtypeStruct((bsz, nslab, s, LANES), F32),
                   jax.ShapeDtypeStruct((bsz, s, d_in - d_ssm), BF16)]
                  + [jax.ShapeDtypeStruct(w.shape, BF16) for w in later_weights],
        compiler_params=_cparams(("arbitrary", "arbitrary")),
        name="inproj",
    )(x, sc, sh, g.reshape(1, d), w_bf16, *later_weights)


def _tables_kernel(lp_ref, lt_ref, bt_ref, ct_ref, d_ref, wglu_ref, erep_ref, etile_ref, w_ref,
                   m_ref, p_ref, q_ref, wg_ref, wb_ref):
    wb_ref[...] = w_ref[...].astype(wb_ref.dtype)
    hp = lax.Precision.HIGHEST
    gb = lp_ref.shape[0]
    L = lp_ref.shape[2] - 1
    n = lp_ref.shape[3]
    c = bt_ref.shape[2]
    w = L * c
    lane = lax.broadcasted_iota(jnp.int32, (c, w), 1)
    row = lax.broadcasted_iota(jnp.int32, (c, w), 0)

    def spread(x, e_ref):
        hi = x.astype(BF16)
        r = x - hi.astype(F32)
        mid = r.astype(BF16)
        lo = (r - mid.astype(F32)).astype(BF16)
        e = e_ref[...]
        return (jnp.dot(hi, e, preferred_element_type=F32) + jnp.dot(mid, e, preferred_element_type=F32)
                + jnp.dot(lo, e, preferred_element_type=F32))

    for g in range(gb):
        lq = [jnp.concatenate([jnp.broadcast_to(lp_ref[g, k, L - 1 - s:L - s, :], (c, n)) for s in range(L)],
                              axis=0) for k in range(2)]
        bt = [jnp.concatenate([bt_ref[g, k]] * L, axis=0) for k in range(2)]
        p_ref[g, :, :n] = (lq[0] * bt[0] - lq[1] * bt[1]).astype(p_ref.dtype)
        p_ref[g, :, n:] = (lq[0] * bt[1] + lq[1] * bt[0]).astype(p_ref.dtype)
        cx = [spread(ct_ref[g, k], etile_ref) for k in range(2)]
        lx = [spread(lt_ref[g, k], erep_ref) for k in range(2)]
        v0_re = cx[0] * lx[0] - cx[1] * lx[1]
        v0_im = cx[0] * lx[1] + cx[1] * lx[0]
        lam_re = lt_ref[g, 0][:, 1:2]
        lam_im = lt_ref[g, 1][:, 1:2]
        v1_re = v0_re * lam_re - v0_im * lam_im
        v1_im = v0_re * lam_im + v0_im * lam_re
        q_ref[g, :n, :] = v1_re.astype(q_ref.dtype)
        q_ref[g, n:, :] = (-v1_im).astype(q_ref.dtype)
        kc = (jnp.dot(bt_ref[g, 0], v0_re, precision=hp, preferred_element_type=F32)
              - jnp.dot(bt_ref[g, 1], v0_im, precision=hp, preferred_element_type=F32))
        kc = kc + jnp.where(lane == row, d_ref[g], 0.0)
        for s in range(L):
            blk = kc if s == 0 else jnp.where(lane >= s * c, pltpu.roll(kc, s * c, 1), 0.0)
            m_ref[g, s * c:(s + 1) * c, :] = blk.astype(m_ref.dtype)
        pos = lax.rem(pl.program_id(0) * gb + g, w // c)
        wrow = jnp.where(lane // c == pos, spread(wglu_ref[g], etile_ref), 0.0)
        wg_ref[0, g * c:(g + 1) * c, :] = wrow.astype(wg_ref.dtype)


def _tables(lp, bbt, ct, d, w_glu, w_in, gb=8):
    g, _, lp1, n = lp.shape
    wrows = w_in.shape[0] // (g // gb)
    L = lp1 - 1
    c = bbt.shape[2]
    w = L * c
    lt = lp[:, :, :L].transpose(0, 1, 3, 2)
    col = np.arange(w)
    e_rep = jnp.asarray(col[None, :] // c == np.arange(L)[:, None], dtype=BF16)
    e_tile = jnp.asarray(col[None, :] % c == np.arange(c)[:, None], dtype=BF16)
    blk = lambda *s: pl.BlockSpec((gb,) + s, lambda i: (i,) + (0,) * len(s))
    per_slab = w // (gb * c)
    return pl.pallas_call(
        _tables_kernel,
        grid=(g // gb,),
        in_specs=[blk(2, lp1, n), blk(2, n, L), blk(2, c, n), blk(2, n, c), blk(c, 1), blk(c, c),
                  _const_spec((L, w)), _const_spec((c, w)),
                  pl.BlockSpec((wrows, w_in.shape[1]), lambda i: (i, 0))],
        out_specs=[blk(w, w), blk(w, 2 * n), blk(2 * n, w),
                   pl.BlockSpec((1, gb * c, w), lambda i: (i // per_slab, i % per_slab, 0)),
                   pl.BlockSpec((wrows, w_in.shape[1]), lambda i: (i, 0))],
        out_shape=[jax.ShapeDtypeStruct((g, w, w), BF16), jax.ShapeDtypeStruct((g, w, 2 * n), BF16),
                   jax.ShapeDtypeStruct((g, 2 * n, w), BF16),
                   jax.ShapeDtypeStruct((g * c // w, w, w), BF16),
                   jax.ShapeDtypeStruct(w_in.shape, BF16)],
        compiler_params=_cparams(("arbitrary",)),
        name="tables",
    )(lp, lt, bbt, ct, d, w_glu, e_rep, e_tile, w_in)


def _ssm_tables(log_dt, a_re, a_im, b_re, b_im, c_re, c_im, d, w_glu, w_in):
    L = SSM_L
    G, N, C = b_re.shape
    dt = jnp.exp(log_dt)[:, None]
    ar_dt = a_re * dt
    ai_dt = a_im * dt
    mag = jnp.exp(ar_dt)
    lb_re = mag * jnp.cos(ai_dt)
    lb_im = mag * jnp.sin(ai_dt)
    den = a_re * a_re + a_im * a_im
    nr = lb_re - 1.0
    ni = lb_im
    f_re = ((nr * a_re + ni * a_im) / den)[:, :, None]
    f_im = ((ni * a_re - nr * a_im) / den)[:, :, None]
    bb_re = f_re * b_re - f_im * b_im
    bb_im = f_re * b_im + f_im * b_re
    tau = jnp.arange(L + 1, dtype=F32)[None, :, None]
    pmag = jnp.exp(ar_dt[:, None, :] * tau)
    lp_re = pmag * jnp.cos(ai_dt[:, None, :] * tau)
    lp_im = pmag * jnp.sin(ai_dt[:, None, :] * tau)
    m, p, q, wg, w_in_bf16 = _tables(jnp.stack([lp_re, lp_im], axis=1),
                                     jnp.stack([bb_re, bb_im], axis=1).transpose(0, 1, 3, 2),
                                     jnp.stack([c_re, c_im], axis=1).transpose(0, 1, 3, 2),
                                     d.reshape(G, C, 1), w_glu, w_in)
    l_re = lp_re[:, L]
    l_im = lp_im[:, L]
    a1 = jnp.concatenate([l_re, l_re], axis=-1)
    a2u = jnp.concatenate([-l_im, l_im], axis=-1)
    a2v = jnp.concatenate([l_im, -l_im], axis=-1)
    return (m, p, q, a1, a2u, a2v, wg), w_in_bf16


def _block_swap_perm():
    idx = np.arange(8 * LANES)
    a, b, c = idx // LANES, (idx // SSM_L) % 8, idx % SSM_L
    return jnp.asarray(idx[None, :] == (b * LANES + a * SSM_L + c)[:, None], dtype=BF16)


def _ssm_kernel(zs_ref, perm_ref, m_ref, p_ref, q_ref, a1_ref, a2u_ref, a2v_ref, wg_ref, bg_ref, go_ref,
                o_ref, zf_ref, xu_ref, xv_ref, hp_ref, ys_ref, hu_ref, hv_ref, *, n, pitch, groups):
    L = SSM_L
    d = groups * L
    nslab = d // LANES

    @pl.when(pl.program_id(1) == 0)
    def _():
        hu_ref[...] = jnp.zeros_like(hu_ref)
        hv_ref[...] = jnp.zeros_like(hv_ref)

    def swap_blocks(pieces):
        lhs = jnp.concatenate([jnp.concatenate(ps, axis=1) for ps in pieces], axis=0)
        return jnp.dot(lhs, perm_ref[...], preferred_element_type=F32).astype(BF16)

    for hh in range(2):
        out = swap_blocks([[zs_ref[0, j, pl.ds(8 * hh + tp, n, stride=L), :].astype(BF16)
                            for tp in range(8)] for j in range(nslab)])
        for j in range(nslab):
            for qq in range(8):
                zf_ref[8 * j + qq, :, hh * LANES:(hh + 1) * LANES] = (
                    out[j * n:(j + 1) * n, qq * LANES:(qq + 1) * LANES])

    gx = 64

    def x_body(i, carry):
        for gi in range(gx):
            g = i * gx + gi
            xo = jnp.dot(zf_ref[g], p_ref[g], preferred_element_type=F32)
            r = pl.ds(pl.multiple_of(g * pitch, 8), n)
            xu_ref[r, :] = xo
            xv_ref[r, :] = pltpu.roll(xo, LANES // 2, 1)
        return carry

    lax.fori_loop(0, groups // gx, x_body, 0)

    a1 = a1_ref[...]
    a2u = a2u_ref[...]
    a2v = a2v_ref[...]
    ks = 8

    def scan_body(i, carry):
        hu, hv = carry
        for s in range(ks):
            kk = i * ks + s
            u = xu_ref[pl.ds(kk, groups, stride=pitch), :]
            v = xv_ref[pl.ds(kk, groups, stride=pitch), :]
            hp_ref[pl.ds(kk, groups, stride=pitch), :] = hu
            hu, hv = a1 * hu + a2u * hv + u, a1 * hv + a2v * hu + v
        return hu, hv

    hu, hv = lax.fori_loop(0, n // ks, scan_body, (hu_ref[...], hv_ref[...]))
    hu_ref[...] = hu
    hv_ref[...] = hv

    gy = 64

    def y_body(i, carry):
        for gi in range(gy):
            g = i * gy + gi
            hprev = hp_ref[pl.ds(pl.multiple_of(g * pitch, 8), n), :].astype(BF16)
            y = (jnp.dot(zf_ref[g], m_ref[g], preferred_element_type=F32)
                 + jnp.dot(hprev, q_ref[g], preferred_element_type=F32))
            zf_ref[g] = _gelu_tanh(y).astype(BF16)
        return carry

    lax.fori_loop(0, groups // gy, y_body, 0)

    for hh in range(2):
        out = swap_blocks([[zf_ref[8 * j + qq, :, hh * LANES:(hh + 1) * LANES] for qq in range(8)]
                           for j in range(nslab)])
        for j in range(nslab):
            for tp in range(8):
                ys_ref[:, (8 * hh + tp) * d + j * LANES:(8 * hh + tp) * d + (j + 1) * LANES] = (
                    out[j * n:(j + 1) * n, tp * LANES:(tp + 1) * LANES])

    w2 = 2 * LANES
    for t in range(L):
        yb = ys_ref[:, t * d:(t + 1) * d]
        gate = jnp.concatenate(
            [jnp.dot(yb[:, i * w2:(i + 1) * w2], wg_ref[i], preferred_element_type=F32)
             for i in range(d // w2)], axis=1) + bg_ref[...]
        y = _rms(yb.astype(F32) * jax.nn.sigmoid(gate), go_ref[...])
        for j in range(nslab):
            o_ref[0, j, pl.ds(t, n, stride=L), :] = y[:, j * LANES:(j + 1) * LANES]


def _ssm(zs, tables, b_glu, g_out, sb=1024):
    m, p, q, a1, a2u, a2v, wg = tables
    bsz, nslab, s, _ = zs.shape
    d_ssm = nslab * LANES
    groups = m.shape[0]
    L = SSM_L
    n = sb // L
    pitch = n + 8
    perm = _block_swap_perm()
    kern = functools.partial(_ssm_kernel, n=n, pitch=pitch, groups=groups)
    slabs = pl.BlockSpec((1, nslab, sb, LANES), lambda b, i: (b, 0, i, 0))
    return pl.pallas_call(
        kern,
        grid=(bsz, s // sb),
        in_specs=[slabs,
                  _const_spec(perm.shape),
                  _const_spec(m.shape), _const_spec(p.shape), _const_spec(q.shape),
                  _const_spec(a1.shape), _const_spec(a2u.shape), _const_spec(a2v.shape),
                  _const_spec(wg.shape), _const_spec((1, d_ssm)), _const_spec((1, d_ssm))],
        out_specs=slabs,
        out_shape=jax.ShapeDtypeStruct((bsz, nslab, s, LANES), F32),
        scratch_shapes=[pltpu.VMEM((groups, n, 2 * LANES), BF16),
                        pltpu.VMEM((groups * pitch, LANES), F32),
                        pltpu.VMEM((groups * pitch, LANES), F32),
                        pltpu.VMEM((groups * pitch, LANES), F32),
                        pltpu.VMEM((n, L * d_ssm), BF16),
                        pltpu.VMEM((groups, LANES), F32),
                        pltpu.VMEM((groups, LANES), F32)],
        compiler_params=_cparams(("arbitrary", "arbitrary")),
        name="ssm",
    )(zs, perm, m, p, q, a1, a2u, a2v, wg, b_glu.reshape(1, d_ssm), g_out.reshape(1, d_ssm))


def _sgu_kernel(zuv_ref, lng_ref, lnb_ref, w_ref, bs_ref, go_ref, o_ref, *, heads, hd):
    d = heads * hd
    ts = zuv_ref.shape[1]
    nc = ts // SGU_CHUNK
    for c in range(nc):
        rows = slice(c * SGU_CHUNK, (c + 1) * SGU_CHUNK)
        v = _gelu_tanh(zuv_ref[0, rows, d:].astype(F32))
        mu = jnp.mean(v, axis=-1, keepdims=True)
        vc = v - mu
        v = vc * lax.rsqrt(jnp.mean(vc * vc, axis=-1, keepdims=True) + EPS) * lng_ref[...] + lnb_ref[...]
        vb = v.astype(BF16)
        mixed = jnp.concatenate(
            [jnp.dot(w_ref[h], vb[:, h * hd:(h + 1) * hd], preferred_element_type=F32)
             for h in range(heads)], axis=1)
        u = _gelu_tanh(zuv_ref[0, rows, :d].astype(F32))
        y = u * (mixed + bs_ref[...])
        o_ref[0, rows] = _rms(y, go_ref[...]).astype(o_ref.dtype)


def _sgu(zuv, ln_g, ln_b, w_masked_bf16, b_full, g_out, heads, ts=512):
    bsz, s, d2 = zuv.shape
    d = d2 // 2
    hd = d // heads
    return pl.pallas_call(
        functools.partial(_sgu_kernel, heads=heads, hd=hd),
        grid=(bsz, s // ts),
        in_specs=[pl.BlockSpec((1, ts, d2), lambda b, i: (b, i, 0)),
                  _const_spec((1, d)), _const_spec((1, d)),
                  _const_spec(w_masked_bf16.shape), _const_spec(b_full.shape), _const_spec((1, d))],
        out_specs=pl.BlockSpec((1, ts, d), lambda b, i: (b, i, 0)),
        out_shape=jax.ShapeDtypeStruct((bsz, s, d), BF16),
        compiler_params=_cparams(("arbitrary", "arbitrary")),
        name="sgu",
    )(zuv, ln_g.reshape(1, d), ln_b.reshape(1, d), w_masked_bf16, b_full, g_out.reshape(1, d))


OUTPROJ_PIECES = 4


def _outproj_kernel(ya_ref, yb_ref, x_ref, gt_ref, g_ref, wa_ref, wb_ref, sc_ref, sh_ref, gn_ref,
                    o_ref, h_ref):
    piece = x_ref.shape[1] // OUTPROJ_PIECES
    gg = g_ref[...] * gt_ref[0]
    gs = gn_ref[...] * (1.0 + sc_ref[0])
    for r0 in range(0, x_ref.shape[1], piece):
        rows = slice(r0, r0 + piece)
        ya = jnp.concatenate([ya_ref[0, j, rows] for j in range(ya_ref.shape[1])], axis=1).astype(BF16)
        y = (jnp.dot(ya, wa_ref[...], preferred_element_type=F32)
             + jnp.dot(yb_ref[0, rows], wb_ref[...], preferred_element_type=F32))
        x1 = x_ref[0, rows] + _rms(y, gg)
        o_ref[0, rows] = x1
        h_ref[0, rows] = (_rms(x1, gs) + sh_ref[0]).astype(h_ref.dtype)


def _outproj(ya, yb, x, gt, g, w_bf16, sc_next, sh_next, g_next, tm=512):
    bsz, s, d = x.shape
    nslab = ya.shape[1]
    db = yb.shape[-1]
    assert nslab * LANES == db and w_bf16.shape[0] == 2 * db
    row = lambda w: pl.BlockSpec((1, tm, w), lambda b, i: (b, i, 0))
    per_batch = pl.BlockSpec((1, 1, d), lambda b, i: (b, 0, 0))
    half = lambda k: pl.BlockSpec((db, d), lambda b, i: (k, 0), pipeline_mode=pl.Buffered(1))
    return pl.pallas_call(
        _outproj_kernel,
        grid=(bsz, s // tm),
        in_specs=[pl.BlockSpec((1, nslab, tm, LANES), lambda b, i: (b, 0, i, 0)), row(db), row(d),
                  per_batch, _const_spec((1, d)), half(0), half(1),
                  per_batch, per_batch, _const_spec((1, d))],
        out_specs=[row(d), row(d)],
        out_shape=[jax.ShapeDtypeStruct((bsz, s, d), F32), jax.ShapeDtypeStruct((bsz, s, d), BF16)],
        compiler_params=_cparams(("arbitrary", "arbitrary")),
        name="outproj",
    )(ya, yb, x, gt, g.reshape(1, d), w_bf16, w_bf16, sc_next, sh_next, g_next.reshape(1, d))


FFN_HALO = 16
def _ffn_up_kernel(hin_ref, hh_ref, wa_ref, wb_ref, cwa_ref, cwb_ref, cba_ref, cbb_ref, o_ref, h_ref):
    i = pl.program_id(1)
    j = pl.program_id(2)

    @pl.when(j == 0)
    def _():
        halo = hh_ref[0]
        h_ref[:FFN_HALO, :] = jnp.where(i > 0, halo, jnp.zeros_like(halo))
        h_ref[FFN_HALO:, :] = hin_ref[0]

    h = h_ref[...]

    def branch(w_ref, cw_ref, cb_ref):
        up = jnp.dot(h, w_ref[...], preferred_element_type=F32)
        cw = cw_ref[...]
        r = (cw[0:1] * pltpu.roll(up, 2, 0) + cw[1:2] * pltpu.roll(up, 1, 0) + cw[2:3] * up)
        return r[FFN_HALO:] + cb_ref[...]

    a = branch(wa_ref, cwa_ref, cba_ref)
    b = branch(wb_ref, cwb_ref, cbb_ref)
    o_ref[0] = (a * jax.nn.sigmoid(a) * b).astype(o_ref.dtype)


def _ffn_up(h, w_up_bf16, conv_w, conv_b, tm=1024, tf=512):
    bsz, s, d = h.shape
    f = w_up_bf16.shape[1] // 2
    nf = f // tf
    hb = tm // FFN_HALO
    cb = conv_b.reshape(1, 2 * f)
    return pl.pallas_call(
        _ffn_up_kernel,
        grid=(bsz, s // tm, nf),
        in_specs=[pl.BlockSpec((1, tm, d), lambda b, i, j: (b, i, 0)),
                  pl.BlockSpec((1, FFN_HALO, d), lambda b, i, j: (b, jnp.maximum(i * hb - 1, 0), 0)),
                  pl.BlockSpec((d, tf), lambda b, i, j: (0, j)),
                  pl.BlockSpec((d, tf), lambda b, i, j: (0, nf + j)),
                  pl.BlockSpec((3, tf), lambda b, i, j: (0, j)),
                  pl.BlockSpec((3, tf), lambda b, i, j: (0, nf + j)),
                  pl.BlockSpec((1, tf), lambda b, i, j: (0, j)),
                  pl.BlockSpec((1, tf), lambda b, i, j: (0, nf + j))],
        out_specs=pl.BlockSpec((1, tm, tf), lambda b, i, j: (b, i, j)),
        out_shape=jax.ShapeDtypeStruct((bsz, s, f), BF16),
        scratch_shapes=[pltpu.VMEM((FFN_HALO + tm, d), BF16)],
        compiler_params=_cparams(("arbitrary", "arbitrary", "arbitrary")),
        name="ffn_up",
    )(h, h, w_up_bf16, w_up_bf16, conv_w, conv_w, cb, cb)


FFN_DOWN_PIECES = 2


def _ffn_down_kernel(act_ref, x_ref, gt_ref, g_ref, wd_ref, o_ref):
    gg = g_ref[...] * gt_ref[0]
    piece = x_ref.shape[1] // FFN_DOWN_PIECES
    for r0 in range(0, x_ref.shape[1], piece):
        rows = slice(r0, r0 + piece)
        f = jnp.dot(act_ref[0, rows], wd_ref[...], preferred_element_type=F32)
        o_ref[0, rows] = x_ref[0, rows] + _rms(f, gg)


def _ffn_down(act, x, gt, g_post, w_down_bf16, tm=512):
    bsz, s, d = x.shape
    f = act.shape[-1]
    return pl.pallas_call(
        _ffn_down_kernel,
        grid=(bsz, s // tm),
        in_specs=[pl.BlockSpec((1, tm, f), lambda b, i: (b, i, 0)),
                  pl.BlockSpec((1, tm, d), lambda b, i: (b, i, 0)),
                  pl.BlockSpec((1, 1, d), lambda b, i: (b, 0, 0)),
                  _const_spec((1, d)), _const_spec((f, d))],
        out_specs=pl.BlockSpec((1, tm, d), lambda b, i: (b, i, 0)),
        out_shape=jax.ShapeDtypeStruct((bsz, s, d), F32),
        compiler_params=_cparams(("arbitrary", "arbitrary")),
        name="ffn_down",
    )(act, x, gt, g_post.reshape(1, d), w_down_bf16)


def kernel(x, c, w_ada, b_ada, g_pre_mix, g_post_mix, w_in, ssm_log_dt, ssm_a_re, ssm_a_im, ssm_b_re, ssm_b_im, ssm_c_re, ssm_c_im, ssm_d, ssm_w_glu, ssm_b_glu, sgu_ln_g, sgu_ln_b, sgu_w, sgu_b, g_out_ssm, g_out_sgu, w_out, g_pre_ffn, g_post_ffn, w_up, conv_w, conv_b, w_down):
    depth = w_ada.shape[0]
    d = x.shape[-1]
    d_ssm = ssm_d.shape[-1]
    heads = sgu_w.shape[1]
    hd = (w_out.shape[1] - d_ssm) // heads
    for l in range(depth):
        mod = _ada(c, w_ada[l], b_ada[l])
        sh1, sc1, gt1, sh2, sc2, gt2 = [mod[:, None, k * d:(k + 1) * d] for k in range(6)]

        tables, wi = _ssm_tables(ssm_log_dt[l], ssm_a_re[l], ssm_a_im[l], ssm_b_re[l], ssm_b_im[l],
                                 ssm_c_re[l], ssm_c_im[l], ssm_d[l], ssm_w_glu[l], w_in[l])
        zs, zuv, wo, wu, wd = _inproj(x, sc1, sh1, g_pre_mix[l], wi, d_ssm,
                                      (w_out[l], w_up[l], w_down[l]))
        y_ssm = _ssm(zs, tables, ssm_b_glu[l], g_out_ssm[l])
        w_s = jnp.tril(sgu_w[l]).astype(BF16)
        b_full = jnp.repeat(sgu_b[l].T, hd, axis=1)
        y_sgu = _sgu(zuv, sgu_ln_g[l], sgu_ln_b[l], w_s, b_full, g_out_sgu[l], heads)
        x, h = _outproj(y_ssm, y_sgu, x, gt1, g_post_mix[l], wo, sc2, sh2, g_pre_ffn[l])

        act = _ffn_up(h, wu, conv_w[l], conv_b[l])
        x = _ffn_down(act, x, gt2, g_post_ffn[l], wd)
    return x
```
